```python
import math
import jax, jax.numpy as jnp
from jax import lax
import numpy as np

D_MODEL = 2048
BATCH = 16
SEQ = 256
DEPTH = 2
DEC_BATCH = 4
DEC_SEQ = 4096
PAST_LEN = 512

GRID_W = 64
N_DIR = 2
EPS = 1e-6
CHUNK = 64
CONV_TAPS = 5
GDN_HEADS = 4
GDN_DK = 128
GDN_DV = 128
GDN_WIDTH = GDN_HEADS * GDN_DV
ATT_HEADS = 8
ATT_KV_HEADS = 2
HEAD_DIM = 128
ATT_GROUP = ATT_HEADS // ATT_KV_HEADS
ATT_WIDTH = ATT_HEADS * HEAD_DIM
ATT_BLOCK = 128
ROPE_THETA = 10000.0
ROPE_AXIS_PAIRS = HEAD_DIM // 4
SSD_HEADS = 8
SSD_HEAD_DIM = 64
SSD_WIDTH = SSD_HEADS * SSD_HEAD_DIM
SSD_GROUPS = 2
SSD_HEADS_PER_GROUP = SSD_HEADS // SSD_GROUPS
SSD_STATE = 128
DT_MIN = 0.001
DT_MAX = 0.1
MIX_WIDTH = GDN_WIDTH + ATT_WIDTH + SSD_WIDTH
IN_SPLITS = (3 * GDN_WIDTH, GDN_WIDTH, N_DIR * GDN_HEADS, N_DIR * GDN_HEADS,
             ATT_WIDTH, ATT_KV_HEADS * HEAD_DIM, ATT_KV_HEADS * HEAD_DIM,
             SSD_WIDTH, SSD_WIDTH + 2 * SSD_GROUPS * SSD_STATE, N_DIR * SSD_HEADS)
IN_COLS = sum(IN_SPLITS)
N_EXPERTS = 16
N_EXPERT_GROUPS = 4
EXPERTS_PER_GROUP = N_EXPERTS // N_EXPERT_GROUPS
GROUP_SCORE_TOPK = 2
TOP_K = 2
EXPERT_FF = 1024
MOE_BLOCK = 128

kernel_name = 'hybrid_prefix_diffusion_step'


def rms_norm(x, g):
    xf = x.astype(jnp.float32)
    y = xf * lax.rsqrt(jnp.mean(xf * xf, axis=-1, keepdims=True) + EPS)
    return (y * g.astype(jnp.float32)).astype(x.dtype)


def l2_normalize(x):
    xf = x.astype(jnp.float32)
    return xf * lax.rsqrt(jnp.sum(xf * xf, axis=-1, keepdims=True) + EPS)


def centred_dwconv(x, w):
    pad = CONV_TAPS // 2
    return lax.conv_general_dilated(
        x, w[:, None, :].astype(x.dtype), window_strides=(1,), padding=[(pad, pad)],
        dimension_numbers=('NWC', 'WIO', 'NWC'), feature_group_count=x.shape[-1])


def flip_seq(t):
    return jnp.flip(t, axis=1)


def to_chunks(t):
    b, n_tok = t.shape[:2]
    t = t.reshape((b, n_tok // CHUNK, CHUNK) + t.shape[2:])
    return t.transpose((1, 0, 3, 2) + tuple(range(4, t.ndim)))


def from_chunks(t):
    n, b, h, ch = t.shape[:4]
    t = t.transpose((1, 0, 3, 2) + tuple(range(4, t.ndim)))
    return t.reshape((b, n * ch, h) + t.shape[4:])


def intra_chunk_decay(gc):
    incl = jnp.tril(jnp.ones((CHUNK, CHUNK), dtype=bool))
    diff = gc[..., :, None] - gc[..., None, :]
    return jnp.where(incl, jnp.exp(jnp.where(incl, diff, 0.0)), 0.0)


def gated_delta_chunked(q, k, v, log_a, beta, s0):
    qc, kc, vc = to_chunks(q), to_chunks(k), to_chunks(v)
    gc = jnp.cumsum(to_chunks(log_a), axis=-1)
    bc = to_chunks(beta)[..., None]
    decay = intra_chunk_decay(gc)
    kb = kc * bc
    lower = jnp.tril(jnp.einsum('zbhid,zbhjd->zbhij', kb, kc) * decay, -1)
    eye = jnp.eye(CHUNK, dtype=lower.dtype)
    tmat = lax.linalg.triangular_solve(eye + lower, jnp.broadcast_to(eye, lower.shape),
                                       left_side=True, lower=True, unit_diagonal=True)
    u = tmat @ (vc * bc)
    w = tmat @ (kb * jnp.exp(gc)[..., None])
    qk = jnp.einsum('zbhid,zbhjd->zbhij', qc, kc) * decay
    qg = qc * jnp.exp(gc)[..., None]
    g_last = gc[..., -1]
    kg = kc * jnp.exp(g_last[..., None] - gc)[..., None]

    def step(s, inp):
        u_i, w_i, qk_i, qg_i, kg_i, gl_i = inp
        v_new = u_i - w_i @ s
        o_i = qg_i @ s + qk_i @ v_new
        s = s * jnp.exp(gl_i)[..., None, None] + jnp.einsum('bhck,bhcv->bhkv', kg_i, v_new)
        return s, o_i

    s_fin, o = lax.scan(step, s0, (u, w, qk, qg, kg, g_last))
    return from_chunks(o), s_fin


def ssd_chunked(x, bm, cm, dt, a, h0):
    xc = to_chunks(x * dt[..., None])
    bc, cc = to_chunks(bm), to_chunks(cm)
    ac = jnp.cumsum(to_chunks(a), axis=-1)
    y_intra = (jnp.einsum('zbhis,zbhjs->zbhij', cc, bc) * intra_chunk_decay(ac)) @ xc
    a_last = ac[..., -1]
    chunk_states = jnp.einsum('zbhjs,zbhjp->zbhps',
                              bc * jnp.exp(a_last[..., None] - ac)[..., None], xc)
    c_dec = cc * jnp.exp(ac)[..., None]

    def step(h, inp):
        st, cd, al = inp
        y_off = jnp.einsum('bhcs,bhps->bhcp', cd, h)
        return h * jnp.exp(al)[..., None, None] + st, y_off

    h_fin, y_off = lax.scan(step, h0, (chunk_states, c_dec, a_last))
    return from_chunks(y_intra + y_off), h_fin


def bidirectional(scan_fn, shared, per_dir, s0):
    o_f, s_f = scan_fn(*shared, *[t[:, :, 0] for t in per_dir], s0[:, 0])
    o_b, s_b = scan_fn(*[flip_seq(t) for t in shared],
                       *[flip_seq(t[:, :, 1]) for t in per_dir], s0[:, 1])
    return o_f + flip_seq(o_b), jnp.stack([s_f, s_b], axis=1)


def axial_rope(n_tok):
    n_rows = n_tok // GRID_W
    rows = jnp.broadcast_to(jnp.arange(n_rows)[:, None], (n_rows, GRID_W)).reshape(n_tok)
    cols = jnp.broadcast_to(jnp.arange(GRID_W)[None, :], (n_rows, GRID_W)).reshape(n_tok)
    inv = 1.0 / (ROPE_THETA ** (jnp.arange(ROPE_AXIS_PAIRS, dtype=jnp.float32) / ROPE_AXIS_PAIRS))
    ang = jnp.concatenate([rows[:, None] * inv, cols[:, None] * inv], axis=-1)
    return jnp.cos(ang), jnp.sin(ang)


def apply_rope(x, cos, sin):
    x1, x2 = jnp.split(x.astype(jnp.float32), 2, axis=-1)
    c, s = cos[None, :, None, :], sin[None, :, None, :]
    return jnp.concatenate([x1 * c - x2 * s, x1 * s + x2 * c], axis=-1).astype(x.dtype)


def blocked_attention(q, k, v):
    bsz, n_tok = q.shape[:2]
    qb = q.reshape(bsz, n_tok // ATT_BLOCK, ATT_BLOCK, ATT_KV_HEADS, ATT_GROUP, HEAD_DIM)
    qb = qb.transpose(1, 0, 2, 3, 4, 5)
    scale = HEAD_DIM ** -0.5

    def one_block(q_blk):
        s = jnp.einsum('bqhgd,bkhd->bhgqk', q_blk, k).astype(jnp.float32) * scale
        p = jax.nn.softmax(s, axis=-1).astype(v.dtype)
        return jnp.einsum('bhgqk,bkhd->bqhgd', p, v)

    o = lax.map(one_block, qb)
    return o.transpose(1, 0, 2, 3, 4, 5).reshape(bsz, n_tok, ATT_HEADS, HEAD_DIM)


def split_in(proj):
    cuts = np.cumsum(IN_SPLITS)[:-1].tolist()
    return jnp.split(proj, cuts, axis=-1)


def token_mixers(h, lp, ctx):
    f32 = jnp.float32
    bsz, n_tok, _ = h.shape
    (g_qkv, g_gate, g_beta, g_decay, a_q, a_k, a_v, s_z, s_xbc, s_dt) = split_in(h @ lp['w_in'])

    qkv = jax.nn.silu(centred_dwconv(g_qkv, lp['gdn_conv']))
    gq, gk, gv = jnp.split(qkv, 3, axis=-1)
    gq = l2_normalize(gq.reshape(bsz, n_tok, GDN_HEADS, GDN_DK)) * (GDN_DK ** -0.5)
    gk = l2_normalize(gk.reshape(bsz, n_tok, GDN_HEADS, GDN_DK))
    gv = gv.reshape(bsz, n_tok, GDN_HEADS, GDN_DV).astype(f32)
    beta = jax.nn.sigmoid(g_beta.astype(f32)).reshape(bsz, n_tok, N_DIR, GDN_HEADS)
    log_a = -jnp.exp(lp['gdn_a_log'].astype(f32)) * jax.nn.softplus(
        g_decay.astype(f32).reshape(bsz, n_tok, N_DIR, GDN_HEADS) + lp['gdn_dt_bias'].astype(f32))
    if ctx is None:
        s_gdn0 = jnp.zeros((bsz, N_DIR, GDN_HEADS, GDN_DK, GDN_DV), f32)
        s_ssd0 = jnp.zeros((bsz, N_DIR, SSD_HEADS, SSD_HEAD_DIM, SSD_STATE), f32)
    else:
        s_gdn0 = ctx[2].astype(f32)
        s_ssd0 = ctx[3].astype(f32)
    o_gdn, s_gdn = bidirectional(gated_delta_chunked, (gq, gk, gv), (log_a, beta), s_gdn0)
    gate_a = jax.nn.silu(g_gate.astype(f32)).reshape(bsz, n_tok, GDN_HEADS, GDN_DV)
    gdn_out = (rms_norm(o_gdn, lp['gdn_norm']) * gate_a).reshape(bsz, n_tok, GDN_WIDTH)

    qa = rms_norm(a_q.reshape(bsz, n_tok, ATT_HEADS, HEAD_DIM), lp['q_norm'])
    ka = rms_norm(a_k.reshape(bsz, n_tok, ATT_KV_HEADS, HEAD_DIM), lp['k_norm'])
    va = a_v.reshape(bsz, n_tok, ATT_KV_HEADS, HEAD_DIM)
    if ctx is None:
        att = blocked_attention(qa, ka, va)
    else:
        cos, sin = axial_rope(n_tok)
        k_all = jnp.concatenate([apply_rope(ka, cos, sin), ctx[0].astype(ka.dtype)], axis=1)
        v_all = jnp.concatenate([va, ctx[1].astype(va.dtype)], axis=1)
        att = blocked_attention(apply_rope(qa, cos, sin), k_all, v_all)
    att_out = att.reshape(bsz, n_tok, ATT_WIDTH)

    xbc = jax.nn.silu(centred_dwconv(s_xbc, lp['ssd_conv_w']) + lp['ssd_conv_b']).astype(f32)
    xs, bm, cm = jnp.split(xbc, [SSD_WIDTH, SSD_WIDTH + SSD_GROUPS * SSD_STATE], axis=-1)
    xs = xs.reshape(bsz, n_tok, SSD_HEADS, SSD_HEAD_DIM)
    bm = jnp.repeat(bm.reshape(bsz, n_tok, SSD_GROUPS, SSD_STATE), SSD_HEADS_PER_GROUP, axis=2)
    cm = jnp.repeat(cm.reshape(bsz, n_tok, SSD_GROUPS, SSD_STATE), SSD_HEADS_PER_GROUP, axis=2)
    dt = jax.nn.softplus(s_dt.astype(f32).reshape(bsz, n_tok, N_DIR, SSD_HEADS)
                         + lp['ssd_dt_bias'].astype(f32))
    a = dt * (-jnp.exp(lp['ssd_a_log'].astype(f32)))
    y_ssd, s_ssd = bidirectional(ssd_chunked, (xs, bm, cm), (dt, a), s_ssd0)
    y_ssd = y_ssd + lp['ssd_d'].astype(f32)[:, None] * xs
    y_ssd = y_ssd.reshape(bsz, n_tok, SSD_WIDTH) * jax.nn.silu(s_z.astype(f32))
    ssd_out = rms_norm(y_ssd.reshape(bsz, n_tok, SSD_GROUPS, SSD_WIDTH // SSD_GROUPS),
                       lp['ssd_norm'].reshape(SSD_GROUPS, SSD_WIDTH // SSD_GROUPS))
    ssd_out = ssd_out.reshape(bsz, n_tok, SSD_WIDTH)

    mixed = jnp.concatenate([gdn_out.astype(h.dtype), att_out.astype(h.dtype),
                             ssd_out.astype(h.dtype)], axis=-1)
    new_ctx = (ka, va, s_gdn, s_ssd) if ctx is None else None
    return mixed @ lp['w_out'], new_ctx


def expert_dispatch(hf, e_idx, gate, w_gate, w_up, w_down):
    n_tok = hf.shape[0]
    n_assign = n_tok * TOP_K
    flat_e = e_idx.reshape(n_assign)
    flat_tok = jnp.repeat(jnp.arange(n_tok, dtype=jnp.int32), TOP_K)
    flat_w = gate.reshape(n_assign)
    order = jnp.argsort(flat_e)
    sorted_e = flat_e[order]
    counts = jnp.zeros((N_EXPERTS,), jnp.int32).at[flat_e].add(1)
    padded = (counts + MOE_BLOCK - 1) // MOE_BLOCK * MOE_BLOCK
    start = jnp.cumsum(counts) - counts
    padded_end = jnp.cumsum(padded)
    padded_start = padded_end - padded
    dest = padded_start[sorted_e] + jnp.arange(n_assign, dtype=jnp.int32) - start[sorted_e]
    n_blocks = -(-n_assign // MOE_BLOCK) + N_EXPERTS
    n_rows = n_blocks * MOE_BLOCK
    row_tok = jnp.full((n_rows,), n_tok, jnp.int32).at[dest].set(flat_tok[order])
    row_w = jnp.zeros((n_rows,), flat_w.dtype).at[dest].set(flat_w[order])
    block_e = jnp.minimum(
        jnp.searchsorted(padded_end, jnp.arange(n_blocks, dtype=jnp.int32) * MOE_BLOCK, side='right'),
        N_EXPERTS - 1)
    h_pad = jnp.concatenate([hf, jnp.zeros((1, D_MODEL), hf.dtype)], axis=0)
    x_rows = h_pad[row_tok].reshape(n_blocks, MOE_BLOCK, D_MODEL)

    def expert_block(args):
        xb, e = args
        return (jax.nn.silu(xb @ w_gate[e]) * (xb @ w_up[e])) @ w_down[e]

    y_rows = lax.map(expert_block, (x_rows, block_e)).reshape(n_rows, D_MODEL)
    y = jnp.zeros((n_tok + 1, D_MODEL), y_rows.dtype).at[row_tok].add(
        y_rows * row_w[:, None].astype(y_rows.dtype))
    return y[:n_tok]


def moe_ffn(h, w_router, b_router, w_gate, w_up, w_down):
    lead = h.shape[:-1]
    hf = h.reshape(-1, D_MODEL)
    n_tok = hf.shape[0]
    scores = jax.nn.sigmoid((hf @ w_router).astype(jnp.float32))
    sel = scores + b_router.astype(jnp.float32)
    group_score = lax.top_k(sel.reshape(n_tok, N_EXPERT_GROUPS, EXPERTS_PER_GROUP),
                            GROUP_SCORE_TOPK)[0].sum(-1)
    best_group = jnp.argmax(group_score, axis=-1)
    in_group = (jnp.arange(N_EXPERTS) // EXPERTS_PER_GROUP)[None, :] == best_group[:, None]
    _, e_idx = lax.top_k(jnp.where(in_group, sel, -jnp.inf), TOP_K)
    gate = jnp.take_along_axis(scores, e_idx, axis=-1)
    gate = gate / jnp.sum(gate, axis=-1, keepdims=True)
    y = expert_dispatch(hf, e_idx, gate, w_gate, w_up, w_down)
    return y.reshape(lead + (D_MODEL,))


def trunk_layer(x, mod, lp, w_router, b_router, ctx):
    shift1, scale1, gate1, shift2, scale2, gate2 = jnp.split(mod, 6, axis=-1)
    h = rms_norm(x, lp['norm_mix']) * (1 + scale1) + shift1
    mixed, new_ctx = token_mixers(h, lp, ctx)
    x = x + gate1 * mixed
    h = rms_norm(x, lp['norm_ffn']) * (1 + scale2) + shift2
    x = x + gate2 * moe_ffn(h, w_router, b_router, lp['w_gate'], lp['w_up'], lp['w_down'])
    return x, new_ctx


def setup_inputs(seed: int = 0) -> dict:
    key = jax.random.key(seed)
    ks = jax.random.split(key, 32)
    f32 = jnp.float32

    def nrm(i, shape, scale):
        return jax.random.normal(ks[i], shape, f32) * scale

    def a_log(i, shape):
        return jnp.log(jax.random.uniform(ks[i], shape, f32, 1.0, 16.0))

    def dt_bias(i, shape):
        dt = jnp.exp(jax.random.uniform(ks[i], shape, f32, math.log(DT_MIN), math.log(DT_MAX)))
        return dt + jnp.log(-jnp.expm1(-dt))

    return {
        'x_prompt': nrm(0, (BATCH, SEQ, D_MODEL), 1.0),
        'x_sample': nrm(1, (DEC_BATCH, DEC_SEQ, D_MODEL), 1.0),
        'c': nrm(2, (DEC_BATCH, D_MODEL), 1.0),
        'cache_k': nrm(3, (DEC_BATCH, DEPTH, PAST_LEN, ATT_KV_HEADS, HEAD_DIM), 1.0),
        'cache_v': nrm(4, (DEC_BATCH, DEPTH, PAST_LEN, ATT_KV_HEADS, HEAD_DIM), 1.0),
        'state_gdn': nrm(5, (DEC_BATCH, DEPTH, N_DIR, GDN_HEADS, GDN_DK, GDN_DV), 0.1),
        'state_ssd': nrm(6, (DEC_BATCH, DEPTH, N_DIR, SSD_HEADS, SSD_HEAD_DIM, SSD_STATE), 0.1),
        'c_ctx': nrm(7, (D_MODEL,), 1.0),
        'w_ada': nrm(8, (DEPTH, D_MODEL, 6 * D_MODEL), 0.25 * D_MODEL ** -0.5),
        'b_ada': nrm(9, (DEPTH, 6 * D_MODEL), 0.02),
        'norm_mix': 1.0 + nrm(10, (DEPTH, D_MODEL), 0.02),
        'norm_ffn': 1.0 + nrm(11, (DEPTH, D_MODEL), 0.02),
        'w_in': nrm(12, (DEPTH, D_MODEL, IN_COLS), D_MODEL ** -0.5),
        'gdn_conv': nrm(13, (DEPTH, CONV_TAPS, 3 * GDN_WIDTH), CONV_TAPS ** -0.5),
        'gdn_a_log': a_log(14, (DEPTH, N_DIR, GDN_HEADS)),
        'gdn_dt_bias': dt_bias(15, (DEPTH, N_DIR, GDN_HEADS)),
        'gdn_norm': 1.0 + nrm(16, (DEPTH, GDN_DV), 0.02),
        'q_norm': 1.0 + nrm(17, (DEPTH, HEAD_DIM), 0.02),
        'k_norm': 1.0 + nrm(18, (DEPTH, HEAD_DIM), 0.02),
        'ssd_conv_w': nrm(19, (DEPTH, CONV_TAPS, SSD_WIDTH + 2 * SSD_GROUPS * SSD_STATE), CONV_TAPS ** -0.5),
        'ssd_conv_b': nrm(20, (DEPTH, SSD_WIDTH + 2 * SSD_GROUPS * SSD_STATE), 0.02),
        'ssd_a_log': a_log(21, (DEPTH, N_DIR, SSD_HEADS)),
        'ssd_dt_bias': dt_bias(22, (DEPTH, N_DIR, SSD_HEADS)),
        'ssd_d': 1.0 + nrm(23, (DEPTH, SSD_HEADS), 0.1),
        'ssd_norm': 1.0 + nrm(24, (DEPTH, SSD_WIDTH), 0.02),
        'w_out': nrm(25, (DEPTH, MIX_WIDTH, D_MODEL), MIX_WIDTH ** -0.5),
        'w_router': nrm(26, (D_MODEL, N_EXPERTS), D_MODEL ** -0.5),
        'b_router': nrm(27, (N_EXPERTS,), 0.01),
        'w_gate': nrm(28, (DEPTH, N_EXPERTS, D_MODEL, EXPERT_FF), D_MODEL ** -0.5),
        'w_up': nrm(29, (DEPTH, N_EXPERTS, D_MODEL, EXPERT_FF), D_MODEL ** -0.5),
        'w_down': nrm(30, (DEPTH, N_EXPERTS, EXPERT_FF, D_MODEL), EXPERT_FF ** -0.5),
    }


def reference(x_prompt, x_sample, c, cache_k, cache_v, state_gdn, state_ssd, c_ctx,
              w_ada, b_ada, norm_mix, norm_ffn, w_in, gdn_conv, gdn_a_log, gdn_dt_bias,
              gdn_norm, q_norm, k_norm, ssd_conv_w, ssd_conv_b, ssd_a_log, ssd_dt_bias,
              ssd_d, ssd_norm, w_out, w_router, b_router, w_gate, w_up, w_down):
    y_prompt = x_prompt
    y_sample = x_sample
    new_k, new_v, new_gdn, new_ssd = [], [], [], []
    for l in range(DEPTH):
        lp = {
            'norm_mix': norm_mix[l], 'norm_ffn': norm_ffn[l], 'w_in': w_in[l],
            'gdn_conv': gdn_conv[l], 'gdn_a_log': gdn_a_log[l], 'gdn_dt_bias': gdn_dt_bias[l],
            'gdn_norm': gdn_norm[l], 'q_norm': q_norm[l], 'k_norm': k_norm[l],
            'ssd_conv_w': ssd_conv_w[l], 'ssd_conv_b': ssd_conv_b[l], 'ssd_a_log': ssd_a_log[l],
            'ssd_dt_bias': ssd_dt_bias[l], 'ssd_d': ssd_d[l], 'ssd_norm': ssd_norm[l],
            'w_out': w_out[l], 'w_gate': w_gate[l], 'w_up': w_up[l], 'w_down': w_down[l],
        }
        mod_ctx = (jax.nn.silu(c_ctx) @ w_ada[l] + b_ada[l])[None, None, :]
        mod_lat = (jax.nn.silu(c) @ w_ada[l] + b_ada[l])[:, None, :]
        y_prompt, (k_l, v_l, g_l, s_l) = trunk_layer(y_prompt, mod_ctx, lp, w_router, b_router, None)
        new_k.append(k_l)
        new_v.append(v_l)
        new_gdn.append(g_l)
        new_ssd.append(s_l)
        ctx = (cache_k[:, l], cache_v[:, l], state_gdn[:, l], state_ssd[:, l])
        y_sample, _ = trunk_layer(y_sample, mod_lat, lp, w_router, b_router, ctx)
    new_cache_k = jnp.stack(new_k, axis=1)
    new_cache_v = jnp.stack(new_v, axis=1)
    new_state_gdn = jnp.stack(new_gdn, axis=1)
    new_state_ssd = jnp.stack(new_ssd, axis=1)
    return (y_prompt, y_sample, new_cache_k, new_cache_v, new_state_gdn, new_state_ssd)
```

```python
import functools

import numpy as np
import jax
import jax.numpy as jnp
from jax import lax
from jax.experimental import pallas as pl
from jax.experimental.pallas import tpu as pltpu

F32 = jnp.float32
BF16 = jnp.bfloat16

D_MODEL = 2048
DEPTH = 2
GRID_W = 64
N_DIR = 2
EPS = 1e-6
CONV_TAPS = 5
GDN_HEADS = 4
GDN_DK = 128
GDN_DV = 128
GDN_WIDTH = GDN_HEADS * GDN_DV
ATT_HEADS = 8
ATT_KV_HEADS = 2
HEAD_DIM = 128
ATT_GROUP = ATT_HEADS // ATT_KV_HEADS
ATT_WIDTH = ATT_HEADS * HEAD_DIM
ROPE_THETA = 10000.0
ROPE_AXIS_PAIRS = HEAD_DIM // 4
SSD_HEADS = 8
SSD_HEAD_DIM = 64
SSD_WIDTH = SSD_HEADS * SSD_HEAD_DIM
SSD_GROUPS = 2
SSD_HEADS_PER_GROUP = SSD_HEADS // SSD_GROUPS
SSD_GROUP_WIDTH = SSD_WIDTH // SSD_GROUPS
SSD_STATE = 128
MIX_WIDTH = GDN_WIDTH + ATT_WIDTH + SSD_WIDTH
N_EXPERTS = 16
N_EXPERT_GROUPS = 4
EXPERTS_PER_GROUP = N_EXPERTS // N_EXPERT_GROUPS
TOP_K = 2
EXPERT_FF = 1024

LANES = 128
SUBLANES = 8
CHUNK = 128
VMEM_LIMIT = 56 * 1024 * 1024

OFF_GQKV = 0
OFF_GGATE = 3 * GDN_WIDTH
OFF_AQ = OFF_GGATE + GDN_WIDTH
OFF_AK = OFF_AQ + ATT_WIDTH
OFF_AV = OFF_AK + ATT_KV_HEADS * HEAD_DIM
OFF_SZ = OFF_AV + ATT_KV_HEADS * HEAD_DIM
OFF_SXBC = OFF_SZ + SSD_WIDTH
MAIN_COLS = OFF_SXBC + SSD_WIDTH + 2 * SSD_GROUPS * SSD_STATE
SM_BETA = 0
SM_DECAY = SM_BETA + N_DIR * GDN_HEADS
SM_DT = SM_DECAY + N_DIR * GDN_HEADS
SM_USED = SM_DT + N_DIR * SSD_HEADS


def _silu(x):
    return x * jax.nn.sigmoid(x)


def _softplus(x):
    return jnp.maximum(x, 0.0) + jnp.log1p(jnp.exp(-jnp.abs(x)))


def _dot(a, b):
    return jnp.dot(a.astype(BF16), b.astype(BF16), preferred_element_type=F32)


def _dot_nt(a, b):
    return lax.dot_general(a.astype(BF16), b.astype(BF16), (((1,), (1,)), ((), ())),
                           preferred_element_type=F32)


def _dot_tn(a, b):
    return lax.dot_general(a.astype(BF16), b.astype(BF16), (((0,), (0,)), ((), ())),
                           preferred_element_type=F32)


def _split3(x):
    hi = x.astype(BF16)
    r1 = x - hi.astype(F32)
    mid = r1.astype(BF16)
    lo = (r1 - mid.astype(F32)).astype(BF16)
    return hi, mid, lo


def _params(*sem):
    return pltpu.CompilerParams(dimension_semantics=sem, vmem_limit_bytes=VMEM_LIMIT)


def _mod_kernel(c_ref, w_ref, b_ref, o_ref):
    s = _silu(c_ref[...])
    o_ref[...] = _dot(s, w_ref[...]) + b_ref[...]


def ada_modulation(cond, w_ada, b_ada, tn=1024):
    n_out = 6 * D_MODEL
    return pl.pallas_call(
        _mod_kernel,
        grid=(DEPTH, n_out // tn),
        in_specs=[
            pl.BlockSpec((SUBLANES, D_MODEL), lambda l, j: (0, 0)),
            pl.BlockSpec((None, D_MODEL, tn), lambda l, j: (l, 0, j)),
            pl.BlockSpec((None, 1, tn), lambda l, j: (l, 0, j)),
        ],
        out_specs=pl.BlockSpec((None, SUBLANES, tn), lambda l, j: (l, 0, j)),
        out_shape=jax.ShapeDtypeStruct((DEPTH, SUBLANES, n_out), F32),
        compiler_params=_params("parallel", "parallel"),
        name="ada_modulation",
    )(cond, w_ada, b_ada.reshape(DEPTH, 1, n_out))


def _inproj_kernel(x_ref, mod_ref, g_ref, w_ref, ws_ref, o_ref, os_ref, h_scr):
    @pl.when(pl.program_id(1) == 0)
    def _():
        x = x_ref[...]
        y = x * lax.rsqrt(jnp.mean(x * x, axis=-1, keepdims=True) + EPS)
        h = (y * g_ref[...]) * (1.0 + mod_ref[1:2, :]) + mod_ref[0:1, :]
        hb = h.astype(BF16)
        h_scr[...] = hb
        os_ref[...] = jnp.dot(hb, ws_ref[...], preferred_element_type=F32)

    o_ref[...] = jnp.dot(h_scr[...], w_ref[...], preferred_element_type=F32)


def in_projection(x, mod, mod_row, g, w_main, w_small, tm, tn=1280):
    n_tok = x.shape[0]
    return pl.pallas_call(
        _inproj_kernel,
        grid=(n_tok // tm, MAIN_COLS // tn),
        in_specs=[
            pl.BlockSpec((tm, D_MODEL), lambda i, j: (i, 0)),
            pl.BlockSpec((None, 6, D_MODEL), lambda i, j: (mod_row(tm)(i), 0, 0)),
            pl.BlockSpec((1, D_MODEL), lambda i, j: (0, 0)),
            pl.BlockSpec((D_MODEL, tn), lambda i, j: (0, j)),
            pl.BlockSpec((D_MODEL, LANES), lambda i, j: (0, 0)),
        ],
        out_specs=[
            pl.BlockSpec((tm, tn), lambda i, j: (i, j)),
            pl.BlockSpec((tm, LANES), lambda i, j: (i, 0)),
        ],
        out_shape=[jax.ShapeDtypeStruct((n_tok, MAIN_COLS), F32),
                   jax.ShapeDtypeStruct((n_tok, LANES), F32)],
        scratch_shapes=[pltpu.VMEM((tm, D_MODEL), BF16)],
        compiler_params=_params("parallel", "arbitrary"),
        name="in_projection",
    )(x, mod, g, w_main, w_small)


def _rms_head(x, g):
    return x * lax.rsqrt(jnp.mean(x * x, axis=-1, keepdims=True) + EPS) * g


def _rope(x, cos2, sin2):
    return x * cos2 + pltpu.roll(x, HEAD_DIM // 2, 1) * sin2


def _attn_kernel(*refs, has_ctx):
    if has_ctx:
        (q_ref, k_ref, v_ref, qn_ref, kn_ref, cq_ref, sq_ref, ck_ref, sk_ref, xk_ref, xv_ref,
         o_ref, k_scr, v_scr) = refs
    else:
        (q_ref, k_ref, v_ref, qn_ref, kn_ref, o_ref, ko_ref, vo_ref, k_scr, v_scr) = refs

    @pl.when(pl.program_id(2) == 0)
    def _():
        kn = _rms_head(k_ref[...], kn_ref[...])
        v = v_ref[...]
        if has_ctx:
            kn = _rope(kn, ck_ref[...], sk_ref[...])
        else:
            ko_ref[...] = kn
            vo_ref[...] = v
        k_scr[...] = kn.astype(BF16)
        v_scr[...] = v.astype(BF16)

    kb = k_scr[...]
    vb = v_scr[...]
    if has_ctx:
        xk = xk_ref[...].astype(BF16)
        xv = xv_ref[...].astype(BF16)
    scale = HEAD_DIM ** -0.5
    for h in range(ATT_GROUP):
        q = _rms_head(q_ref[:, h * HEAD_DIM:(h + 1) * HEAD_DIM], qn_ref[...])
        if has_ctx:
            q = _rope(q, cq_ref[...], sq_ref[...])
        qb = (q * scale).astype(BF16)
        s1 = _dot_nt(qb, kb)
        m = jnp.max(s1, axis=-1, keepdims=True)
        if has_ctx:
            s2 = _dot_nt(qb, xk)
            m = jnp.maximum(m, jnp.max(s2, axis=-1, keepdims=True))
        p1 = jnp.exp(s1 - m)
        den = jnp.sum(p1, axis=-1, keepdims=True)
        acc = jnp.dot(p1.astype(BF16), vb, preferred_element_type=F32)
        if has_ctx:
            p2 = jnp.exp(s2 - m)
            den = den + jnp.sum(p2, axis=-1, keepdims=True)
            acc = acc + jnp.dot(p2.astype(BF16), xv, preferred_element_type=F32)
        o_ref[:, h * HEAD_DIM:(h + 1) * HEAD_DIM] = (acc / den).astype(o_ref.dtype)


def attention(proj, q_norm, k_norm, n_seq, seq_len, tq, ctx=None):
    n_tok = n_seq * seq_len
    nq = seq_len // tq
    gw = ATT_GROUP * HEAD_DIM
    has_ctx = ctx is not None
    in_specs = [
        pl.BlockSpec((tq, gw), lambda b, g, i: (b * nq + i, OFF_AQ // gw + g)),
        pl.BlockSpec((seq_len, HEAD_DIM), lambda b, g, i: (b, OFF_AK // HEAD_DIM + g)),
        pl.BlockSpec((seq_len, HEAD_DIM), lambda b, g, i: (b, OFF_AV // HEAD_DIM + g)),
        pl.BlockSpec((1, HEAD_DIM), lambda b, g, i: (0, 0)),
        pl.BlockSpec((1, HEAD_DIM), lambda b, g, i: (0, 0)),
    ]
    args = [proj, proj, proj, q_norm.reshape(1, HEAD_DIM), k_norm.reshape(1, HEAD_DIM)]
    out_specs = [pl.BlockSpec((tq, gw), lambda b, g, i: (b * nq + i, g))]
    out_shape = [jax.ShapeDtypeStruct((n_tok, ATT_WIDTH), BF16)]
    if has_ctx:
        cache_k4, cache_v4, layer, cos2, sin2 = ctx
        past = cache_k4.shape[2]
        in_specs += [
            pl.BlockSpec((tq, HEAD_DIM), lambda b, g, i: (i, 0)),
            pl.BlockSpec((tq, HEAD_DIM), lambda b, g, i: (i, 0)),
            pl.BlockSpec((seq_len, HEAD_DIM), lambda b, g, i: (0, 0)),
            pl.BlockSpec((seq_len, HEAD_DIM), lambda b, g, i: (0, 0)),
            pl.BlockSpec((None, None, past, HEAD_DIM), lambda b, g, i: (b, layer, 0, g)),
            pl.BlockSpec((None, None, past, HEAD_DIM), lambda b, g, i: (b, layer, 0, g)),
        ]
        args += [cos2, sin2, cos2, sin2, cache_k4, cache_v4]
    else:
        kv_w = ATT_KV_HEADS * HEAD_DIM
        out_specs += [pl.BlockSpec((seq_len, HEAD_DIM), lambda b, g, i: (b, g)),
                      pl.BlockSpec((seq_len, HEAD_DIM), lambda b, g, i: (b, g))]
        out_shape += [jax.ShapeDtypeStruct((n_tok, kv_w), F32),
                      jax.ShapeDtypeStruct((n_tok, kv_w), F32)]
    return pl.pallas_call(
        functools.partial(_attn_kernel, has_ctx=has_ctx),
        grid=(n_seq, ATT_KV_HEADS, nq),
        in_specs=in_specs,
        out_specs=out_specs,
        out_shape=out_shape,
        scratch_shapes=[pltpu.VMEM((seq_len, HEAD_DIM), BF16), pltpu.VMEM((seq_len, HEAD_DIM), BF16)],
        compiler_params=_params("parallel", "parallel", "arbitrary"),
        name="attention_latent" if has_ctx else "attention_context",
    )(*args)


def _iota2(shape, dim):
    return lax.broadcasted_iota(jnp.int32, shape, dim)


def _conv_chunk(x_ref, w, c, n_chunks, bias=None):
    seq_len = n_chunks * CHUNK
    r0 = pl.multiple_of(c * CHUNK, CHUNK)
    cur = x_ref[pl.ds(r0, CHUNK), :]
    lo = pl.multiple_of(jnp.maximum(r0 - SUBLANES, 0), SUBLANES)
    hi = pl.multiple_of(jnp.minimum(r0 + CHUNK, seq_len - SUBLANES), SUBLANES)
    prev = jnp.where(c > 0, x_ref[pl.ds(lo, SUBLANES), :], 0.0)
    nxt = jnp.where(c < n_chunks - 1, x_ref[pl.ds(hi, SUBLANES), :], 0.0)
    ext = jnp.concatenate([prev, cur, nxt], axis=0)
    n_ext = CHUNK + 2 * SUBLANES
    pad = CONV_TAPS // 2
    acc = None
    for j in range(CONV_TAPS):
        shifted = ext if j == pad else pltpu.roll(ext, (pad - j) % n_ext, 0)
        term = shifted[SUBLANES:SUBLANES + CHUNK, :] * w[j:j + 1, :]
        acc = term if acc is None else acc + term
    if bias is not None:
        acc = acc + bias
    return acc


def _chunk_cumsums(a):
    tril = (_iota2((CHUNK, CHUNK), 1) <= _iota2((CHUNK, CHUNK), 0)).astype(BF16)
    hi, mid, lo = _split3(a)
    pre = (jnp.dot(tril, hi, preferred_element_type=F32)
           + jnp.dot(tril, mid, preferred_element_type=F32)
           + jnp.dot(tril, lo, preferred_element_type=F32))
    tot = pre[CHUNK - 1:CHUNK, :]
    suf = tot - pre + a
    return pre, suf, tot


def _pick_col(x, lane_ids, c):
    return jnp.sum(jnp.where(lane_ids == c, x, 0.0), axis=1, keepdims=True)


def _decay_matrix(g_col, g_row, incl):
    d = g_col - g_row
    return jnp.where(incl, jnp.exp(jnp.where(incl, d, 0.0)), 0.0)


def _unit_tri_inverse(lm, lower):
    ii = _iota2((CHUNK, CHUNK), 0)
    jj = _iota2((CHUNK, CHUNK), 1)
    x = None
    s = 1
    while s < CHUNK:
        sh = s.bit_length() - 1
        bi = ii >> sh
        bj = jj >> sh
        same = (bi >> 1) == (bj >> 1)
        if lower:
            m = same & ((bi & 1) == 1) & ((bj & 1) == 0)
        else:
            m = same & ((bi & 1) == 0) & ((bj & 1) == 1)
        off = jnp.where(m, lm, 0.0)
        if x is None:
            x = (ii == jj).astype(F32) - off
        else:
            x = x - _dot(x, _dot(off, x))
        s *= 2
    return x


def _gdn_kernel(*refs, n_chunks, has_state):
    (q_ref, k_ref, v_ref, gate_ref, sm_ref, wq_ref, wk_ref, wv_ref, alog_ref, dtb_ref, gn_ref) = refs[:11]
    pos = 11
    if has_state:
        s0_ref = refs[pos]
        pos += 1
    out_ref, sfin_ref = refs[pos:pos + 2]
    (u_scr, w_scr, qk_scr, qg_scr, kg_scr, egl_scr, o_scr, gct_scr) = refs[pos + 2:]
    h = pl.program_id(1)
    lane_ids = _iota2((CHUNK, LANES), 1)
    lane_row = _iota2((1, LANES), 1)
    ii = _iota2((CHUNK, CHUNK), 0)
    jj = _iota2((CHUNK, CHUNK), 1)
    neg_a = -jnp.exp(alog_ref[...])
    dtb = dtb_ref[...]

    def prep(c, carry):
        r0 = pl.multiple_of(c * CHUNK, CHUNK)
        q = _silu(_conv_chunk(q_ref, wq_ref[...], c, n_chunks))
        k = _silu(_conv_chunk(k_ref, wk_ref[...], c, n_chunks))
        v = _silu(_conv_chunk(v_ref, wv_ref[...], c, n_chunks))
        q = q * lax.rsqrt(jnp.sum(q * q, axis=-1, keepdims=True) + EPS) * (GDN_DK ** -0.5)
        k = k * lax.rsqrt(jnp.sum(k * k, axis=-1, keepdims=True) + EPS)
        sm = sm_ref[pl.ds(r0, CHUNK), :]
        beta_all = jax.nn.sigmoid(sm)
        la_all = neg_a * _softplus(sm + dtb)
        pre, suf, tot = _chunk_cumsums(la_all)
        kk = _dot_nt(k, k)
        qk = _dot_nt(q, k)
        for d in range(N_DIR):
            lower = d == 0
            col = SM_DECAY + d * GDN_HEADS + h
            gc = pre if lower else suf
            gct_scr[...] = gc.T
            g_col = _pick_col(gc, lane_ids, col)
            g_row = gct_scr[pl.ds(col, 1), :]
            beta = _pick_col(beta_all, lane_ids, SM_BETA + d * GDN_HEADS + h)
            g_last = jnp.sum(jnp.where(lane_row == col, tot, 0.0), axis=1, keepdims=True)
            incl = (jj <= ii) if lower else (jj >= ii)
            strict = (jj < ii) if lower else (jj > ii)
            decay = _decay_matrix(g_col, g_row, incl)
            lm = jnp.where(strict, kk * beta * decay, 0.0)
            t = _unit_tri_inverse(lm, lower)
            eg = jnp.exp(g_col)
            u_scr[d, pl.ds(r0, CHUNK), :] = _dot(t, v * beta)
            w_scr[d, pl.ds(r0, CHUNK), :] = _dot(t, k * (beta * eg)).astype(BF16)
            qk_scr[d, pl.ds(r0, CHUNK), :] = (qk * decay).astype(BF16)
            qg_scr[d, pl.ds(r0, CHUNK), :] = (q * eg).astype(BF16)
            kg_scr[d, pl.ds(r0, CHUNK), :] = (k * jnp.exp(g_last - g_col)).astype(BF16)
            egl_scr[d, pl.ds(c, 1), :] = jnp.broadcast_to(jnp.exp(g_last), (1, LANES))
        return carry

    lax.fori_loop(0, n_chunks, prep, 0)
    o_scr[...] = jnp.zeros_like(o_scr)

    def step(d, c, s):
        r0 = pl.multiple_of(c * CHUNK, CHUNK)
        sb = s.astype(BF16)
        v_new = u_scr[d, pl.ds(r0, CHUNK), :] - jnp.dot(w_scr[d, pl.ds(r0, CHUNK), :], sb,
                                                        preferred_element_type=F32)
        vb = v_new.astype(BF16)
        o = (jnp.dot(qg_scr[d, pl.ds(r0, CHUNK), :], sb, preferred_element_type=F32)
             + jnp.dot(qk_scr[d, pl.ds(r0, CHUNK), :], vb, preferred_element_type=F32))
        o_scr[pl.ds(r0, CHUNK), :] += o
        return s * egl_scr[d, pl.ds(c, 1), :] + _dot_tn(kg_scr[d, pl.ds(r0, CHUNK), :], vb)

    def scan(i, carry):
        s_f, s_b = carry
        return step(0, i, s_f), step(1, n_chunks - 1 - i, s_b)

    if has_state:
        init = (s0_ref[0], s0_ref[1])
    else:
        init = (jnp.zeros((GDN_DK, GDN_DV), F32), jnp.zeros((GDN_DK, GDN_DV), F32))
    s_f, s_b = lax.fori_loop(0, n_chunks, scan, init)
    sfin_ref[0] = s_f
    sfin_ref[1] = s_b

    def finish(c, carry):
        r0 = pl.multiple_of(c * CHUNK, CHUNK)
        o = o_scr[pl.ds(r0, CHUNK), :]
        y = o * lax.rsqrt(jnp.mean(o * o, axis=-1, keepdims=True) + EPS) * gn_ref[...]
        out_ref[pl.ds(r0, CHUNK), :] = (y * _silu(gate_ref[pl.ds(r0, CHUNK), :])).astype(out_ref.dtype)
        return carry

    lax.fori_loop(0, n_chunks, finish, 0)


def gated_deltanet(proj, small, conv_w, alog_vec, dtb_vec, gnorm, n_seq, seq_len, state=None):
    n_tok = n_seq * seq_len
    n_chunks = seq_len // CHUNK
    has_state = state is not None
    nh = GDN_HEADS
    in_specs = [
        pl.BlockSpec((seq_len, LANES), lambda b, h: (b, h)),
        pl.BlockSpec((seq_len, LANES), lambda b, h: (b, nh + h)),
        pl.BlockSpec((seq_len, LANES), lambda b, h: (b, 2 * nh + h)),
        pl.BlockSpec((seq_len, LANES), lambda b, h: (b, 3 * nh + h)),
        pl.BlockSpec((seq_len, LANES), lambda b, h: (b, 0)),
        pl.BlockSpec((CONV_TAPS, LANES), lambda b, h: (0, h)),
        pl.BlockSpec((CONV_TAPS, LANES), lambda b, h: (0, nh + h)),
        pl.BlockSpec((CONV_TAPS, LANES), lambda b, h: (0, 2 * nh + h)),
        pl.BlockSpec((1, LANES), lambda b, h: (0, 0)),
        pl.BlockSpec((1, LANES), lambda b, h: (0, 0)),
        pl.BlockSpec((1, LANES), lambda b, h: (0, 0)),
    ]
    args = [proj, proj, proj, proj, small, conv_w, conv_w, conv_w, alog_vec, dtb_vec,
            gnorm.reshape(1, GDN_DV)]
    if has_state:
        state_gdn, layer = state
        in_specs.append(pl.BlockSpec((None, None, N_DIR, None, GDN_DK, GDN_DV),
                                     lambda b, h: (b, layer, 0, h, 0, 0)))
        args.append(state_gdn)
    return pl.pallas_call(
        functools.partial(_gdn_kernel, n_chunks=n_chunks, has_state=has_state),
        grid=(n_seq, GDN_HEADS),
        in_specs=in_specs,
        out_specs=[
            pl.BlockSpec((seq_len, LANES), lambda b, h: (b, h)),
            pl.BlockSpec((None, N_DIR, None, GDN_DK, GDN_DV), lambda b, h: (b, 0, h, 0, 0)),
        ],
        out_shape=[jax.ShapeDtypeStruct((n_tok, GDN_WIDTH), BF16),
                   jax.ShapeDtypeStruct((n_seq, N_DIR, GDN_HEADS, GDN_DK, GDN_DV), F32)],
        scratch_shapes=[
            pltpu.VMEM((N_DIR, seq_len, GDN_DV), F32),
            pltpu.VMEM((N_DIR, seq_len, GDN_DK), BF16),
            pltpu.VMEM((N_DIR, seq_len, CHUNK), BF16),
            pltpu.VMEM((N_DIR, seq_len, GDN_DK), BF16),
            pltpu.VMEM((N_DIR, seq_len, GDN_DK), BF16),
            pltpu.VMEM((N_DIR, max(n_chunks, SUBLANES), LANES), F32),
            pltpu.VMEM((seq_len, GDN_DV), F32),
            pltpu.VMEM((LANES, CHUNK), F32),
        ],
        compiler_params=_params("parallel", "parallel"),
        name="gdn_latent" if has_state else "gdn_context",
    )(*args)


def _ssd_kernel(*refs, n_chunks, has_state):
    (x_ref, b_ref, c_ref, z_ref, sm_ref, wx_ref, wb_ref, wc_ref, bx_ref, bb_ref, bc_ref,
     alog_ref, dtb_ref, dvec_ref, gn_ref) = refs[:15]
    pos = 15
    if has_state:
        h0_ref = refs[pos]
        pos += 1
    out_ref, hfin_ref = refs[pos:pos + 2]
    xs_scr, bs_scr, cs_scr, y_scr, h_scr, act_scr = refs[pos + 2:]
    g = pl.program_id(1)
    hpg = SSD_HEADS_PER_GROUP
    P = SSD_HEAD_DIM
    lane_ids = _iota2((CHUNK, LANES), 1)
    lane_row = _iota2((1, LANES), 1)
    ii = _iota2((CHUNK, CHUNK), 0)
    jj = _iota2((CHUNK, CHUNK), 1)
    neg_a = -jnp.exp(alog_ref[...])
    dtb = dtb_ref[...]

    def prep(c, carry):
        r0 = pl.multiple_of(c * CHUNK, CHUNK)
        xs_scr[pl.ds(r0, CHUNK), :] = _silu(_conv_chunk(x_ref, wx_ref[...], c, n_chunks, bx_ref[...]))
        bs_scr[pl.ds(r0, CHUNK), :] = _silu(_conv_chunk(b_ref, wb_ref[...], c, n_chunks, bb_ref[...]))
        cs_scr[pl.ds(r0, CHUNK), :] = _silu(_conv_chunk(c_ref, wc_ref[...], c, n_chunks, bc_ref[...]))
        return carry

    lax.fori_loop(0, n_chunks, prep, 0)
    y_scr[...] = jnp.zeros_like(y_scr)
    if has_state:
        h_scr[...] = h0_ref[...]
    else:
        h_scr[...] = jnp.zeros_like(h_scr)

    def chunk_dir(d, c):
        r0 = pl.multiple_of(c * CHUNK, CHUNK)
        lower = d == 0
        x = xs_scr[pl.ds(r0, CHUNK), :]
        bm = bs_scr[pl.ds(r0, CHUNK), :]
        cm = cs_scr[pl.ds(r0, CHUNK), :]
        sm = sm_ref[pl.ds(r0, CHUNK), :]
        dt_all = _softplus(sm + dtb)
        a_all = dt_all * neg_a
        pre, suf, tot = _chunk_cumsums(a_all)
        ac = pre if lower else suf
        act_scr[d] = ac.T
        cb = _dot_nt(cm, bm)
        incl = (jj <= ii) if lower else (jj >= ii)
        for hh in range(hpg):
            col = SM_DT + d * SSD_HEADS + g * hpg + hh
            dt = _pick_col(dt_all, lane_ids, col)
            a_col = _pick_col(ac, lane_ids, col)
            a_row = act_scr[d, pl.ds(col, 1), :]
            a_last = jnp.sum(jnp.where(lane_row == col, tot, 0.0), axis=1, keepdims=True)
            xdt = x[:, hh * P:(hh + 1) * P] * dt
            decay = _decay_matrix(a_col, a_row, incl)
            y = _dot(cb * decay, xdt)
            st = _dot_tn(xdt, bm * jnp.exp(a_last - a_col))
            hprev = h_scr[d, hh]
            y = y + _dot_nt(cm * jnp.exp(a_col), hprev)
            h_scr[d, hh] = hprev * jnp.exp(a_last) + st
            y_scr[pl.ds(r0, CHUNK), hh * P:(hh + 1) * P] += y

    def scan(i, carry):
        chunk_dir(0, i)
        chunk_dir(1, n_chunks - 1 - i)
        return carry

    lax.fori_loop(0, n_chunks, scan, 0)
    hfin_ref[...] = h_scr[...]

    def finish(c, carry):
        r0 = pl.multiple_of(c * CHUNK, CHUNK)
        y = y_scr[pl.ds(r0, CHUNK), :] + dvec_ref[...] * xs_scr[pl.ds(r0, CHUNK), :]
        y = y * _silu(z_ref[pl.ds(r0, CHUNK), :])
        y = y * lax.rsqrt(jnp.mean(y * y, axis=-1, keepdims=True) + EPS) * gn_ref[...]
        out_ref[pl.ds(r0, CHUNK), :] = y.astype(out_ref.dtype)
        return carry

    lax.fori_loop(0, n_chunks, finish, 0)


def ssd_mixer(proj, small, conv_w, conv_b, alog_vec, dtb_vec, d_vec, gnorm, n_seq, seq_len, state=None):
    n_tok = n_seq * seq_len
    n_chunks = seq_len // CHUNK
    has_state = state is not None
    gw = SSD_GROUP_WIDTH
    hpg = SSD_HEADS_PER_GROUP
    xb = OFF_SXBC // gw
    bb = (OFF_SXBC + SSD_WIDTH) // LANES
    cb = bb + SSD_GROUPS
    zb = OFF_SZ // gw
    in_specs = [
        pl.BlockSpec((seq_len, gw), lambda b, g: (b, xb + g)),
        pl.BlockSpec((seq_len, LANES), lambda b, g: (b, bb + g)),
        pl.BlockSpec((seq_len, LANES), lambda b, g: (b, cb + g)),
        pl.BlockSpec((seq_len, gw), lambda b, g: (b, zb + g)),
        pl.BlockSpec((seq_len, LANES), lambda b, g: (b, 0)),
        pl.BlockSpec((CONV_TAPS, gw), lambda b, g: (0, g)),
        pl.BlockSpec((CONV_TAPS, LANES), lambda b, g: (0, SSD_WIDTH // LANES + g)),
        pl.BlockSpec((CONV_TAPS, LANES), lambda b, g: (0, SSD_WIDTH // LANES + SSD_GROUPS + g)),
        pl.BlockSpec((1, gw), lambda b, g: (0, g)),
        pl.BlockSpec((1, LANES), lambda b, g: (0, SSD_WIDTH // LANES + g)),
        pl.BlockSpec((1, LANES), lambda b, g: (0, SSD_WIDTH // LANES + SSD_GROUPS + g)),
        pl.BlockSpec((1, LANES), lambda b, g: (0, 0)),
        pl.BlockSpec((1, LANES), lambda b, g: (0, 0)),
        pl.BlockSpec((1, gw), lambda b, g: (0, g)),
        pl.BlockSpec((1, gw), lambda b, g: (0, g)),
    ]
    conv_b2 = conv_b.reshape(1, -1)
    args = [proj, proj, proj, proj, small, conv_w, conv_w, conv_w, conv_b2, conv_b2, conv_b2,
            alog_vec, dtb_vec, d_vec, gnorm.reshape(1, SSD_WIDTH)]
    if has_state:
        state_ssd, layer = state
        in_specs.append(pl.BlockSpec((None, None, N_DIR, hpg, SSD_HEAD_DIM, SSD_STATE),
                                     lambda b, g: (b, layer, 0, g, 0, 0)))
        args.append(state_ssd)
    return pl.pallas_call(
        functools.partial(_ssd_kernel, n_chunks=n_chunks, has_state=has_state),
        grid=(n_seq, SSD_GROUPS),
        in_specs=in_specs,
        out_specs=[
            pl.BlockSpec((seq_len, gw), lambda b, g: (b, g)),
            pl.BlockSpec((None, N_DIR, hpg, SSD_HEAD_DIM, SSD_STATE), lambda b, g: (b, 0, g, 0, 0)),
        ],
        out_shape=[jax.ShapeDtypeStruct((n_tok, SSD_WIDTH), BF16),
                   jax.ShapeDtypeStruct((n_seq, N_DIR, SSD_HEADS, SSD_HEAD_DIM, SSD_STATE), F32)],
        scratch_shapes=[
            pltpu.VMEM((seq_len, gw), F32),
            pltpu.VMEM((seq_len, SSD_STATE), F32),
            pltpu.VMEM((seq_len, SSD_STATE), F32),
            pltpu.VMEM((seq_len, gw), F32),
            pltpu.VMEM((N_DIR, hpg, SSD_HEAD_DIM, SSD_STATE), F32),
            pltpu.VMEM((N_DIR, LANES, CHUNK), F32),
        ],
        compiler_params=_params("parallel", "parallel"),
        name="ssd_latent" if has_state else "ssd_context",
    )(*args)


def _top2_sum(a, b, c, d):
    hi1, lo1 = jnp.maximum(a, b), jnp.minimum(a, b)
    hi2, lo2 = jnp.maximum(c, d), jnp.minimum(c, d)
    return jnp.maximum(hi1, hi2) + jnp.maximum(jnp.minimum(hi1, hi2), jnp.maximum(lo1, lo2))


def _outproj_kernel(x_ref, gdn_ref, att_ref, ssd_ref, mod_ref, g_ref, w_ref, wr_ref, br_ref,
                    xo_ref, h_ref, route_ref):
    mixed = (jnp.dot(gdn_ref[...], w_ref[0:GDN_WIDTH, :], preferred_element_type=F32)
             + jnp.dot(att_ref[...], w_ref[GDN_WIDTH:GDN_WIDTH + ATT_WIDTH, :], preferred_element_type=F32)
             + jnp.dot(ssd_ref[...], w_ref[GDN_WIDTH + ATT_WIDTH:, :], preferred_element_type=F32))
    x = x_ref[...] + mod_ref[2:3, :] * mixed
    xo_ref[...] = x
    y = x * lax.rsqrt(jnp.mean(x * x, axis=-1, keepdims=True) + EPS)
    h = (y * g_ref[...]) * (1.0 + mod_ref[4:5, :]) + mod_ref[3:4, :]
    h_ref[...] = h
    lt = _dot_nt(wr_ref[...], h)
    sc = [jax.nn.sigmoid(lt[e:e + 1, :]) for e in range(N_EXPERTS)]
    sel = [sc[e] + br_ref[e:e + 1, :] for e in range(N_EXPERTS)]
    epg = EXPERTS_PER_GROUP
    gs = [_top2_sum(*sel[gi * epg:(gi + 1) * epg]) for gi in range(N_EXPERT_GROUPS)]
    best, bg = gs[0], jnp.zeros_like(gs[0], dtype=jnp.int32)
    for gi in range(1, N_EXPERT_GROUPS):
        upd = gs[gi] > best
        bg = jnp.where(upd, gi, bg)
        best = jnp.where(upd, gs[gi], best)
    cand_sel, cand_sc = [], []
    for k in range(epg):
        vs, vc = sel[k], sc[k]
        for gi in range(1, N_EXPERT_GROUPS):
            vs = jnp.where(bg == gi, sel[gi * epg + k], vs)
            vc = jnp.where(bg == gi, sc[gi * epg + k], vc)
        cand_sel.append(vs)
        cand_sc.append(vc)
    m1, i1, g1 = cand_sel[0], jnp.zeros_like(bg), cand_sc[0]
    for k in range(1, epg):
        upd = cand_sel[k] > m1
        i1 = jnp.where(upd, k, i1)
        g1 = jnp.where(upd, cand_sc[k], g1)
        m1 = jnp.where(upd, cand_sel[k], m1)
    m2 = jnp.full_like(m1, -jnp.inf)
    i2, g2 = jnp.full_like(bg, -1), jnp.zeros_like(g1)
    for k in range(epg):
        upd = (i1 != k) & ((cand_sel[k] > m2) | (i2 < 0))
        i2 = jnp.where(upd, k, i2)
        g2 = jnp.where(upd, cand_sc[k], g2)
        m2 = jnp.where(upd, cand_sel[k], m2)
    den = g1 + g2
    route_ref[...] = jnp.zeros_like(route_ref)
    route_ref[0:1, :] = (bg * epg + i1).astype(F32)
    route_ref[1:2, :] = (bg * epg + i2).astype(F32)
    route_ref[2:3, :] = g1 / den
    route_ref[3:4, :] = g2 / den


def out_projection(x, gdn, att, ssd, mod, mod_row, g, w_out, w_router, b_router, tm=512):
    n_tok = x.shape[0]
    return pl.pallas_call(
        _outproj_kernel,
        grid=(n_tok // tm,),
        in_specs=[
            pl.BlockSpec((tm, D_MODEL), lambda i: (i, 0)),
            pl.BlockSpec((tm, GDN_WIDTH), lambda i: (i, 0)),
            pl.BlockSpec((tm, ATT_WIDTH), lambda i: (i, 0)),
            pl.BlockSpec((tm, SSD_WIDTH), lambda i: (i, 0)),
            pl.BlockSpec((None, 6, D_MODEL), lambda i: (mod_row(tm)(i), 0, 0)),
            pl.BlockSpec((1, D_MODEL), lambda i: (0, 0)),
            pl.BlockSpec((MIX_WIDTH, D_MODEL), lambda i: (0, 0)),
            pl.BlockSpec((LANES, D_MODEL), lambda i: (0, 0)),
            pl.BlockSpec((LANES, 1), lambda i: (0, 0)),
        ],
        out_specs=[
            pl.BlockSpec((tm, D_MODEL), lambda i: (i, 0)),
            pl.BlockSpec((tm, D_MODEL), lambda i: (i, 0)),
            pl.BlockSpec((SUBLANES, tm), lambda i: (0, i)),
        ],
        out_shape=[jax.ShapeDtypeStruct((n_tok, D_MODEL), F32),
                   jax.ShapeDtypeStruct((n_tok, D_MODEL), F32),
                   jax.ShapeDtypeStruct((SUBLANES, n_tok), F32)],
        compiler_params=_params("parallel"),
        name="out_projection",
    )(x, gdn, att, ssd, mod, g, w_out, w_router, b_router)


def _gather_kernel(idx_ref, src_ref, dst_ref, sem, *, rows):
    base = pl.program_id(0) * rows

    def copy(r):
        return pltpu.make_async_copy(src_ref.at[pl.ds(idx_ref[base + r], 1)],
                                     dst_ref.at[pl.ds(base + r, 1)], sem)

    def issue(r, carry):
        copy(r).start()
        return carry

    def drain(r, carry):
        copy(r).wait()
        return carry

    lax.fori_loop(0, rows, issue, 0)
    lax.fori_loop(0, rows, drain, 0)


def gather_rows(src, idx, rows=512):
    n_rows = idx.shape[0]
    return pl.pallas_call(
        functools.partial(_gather_kernel, rows=rows),
        grid_spec=pltpu.PrefetchScalarGridSpec(
            num_scalar_prefetch=1,
            grid=(n_rows // rows,),
            in_specs=[pl.BlockSpec(memory_space=pl.ANY)],
            out_specs=pl.BlockSpec(memory_space=pl.ANY),
            scratch_shapes=[pltpu.SemaphoreType.DMA(())],
        ),
        out_shape=jax.ShapeDtypeStruct((n_rows, src.shape[1]), src.dtype),
        compiler_params=_params("arbitrary"),
        name="dispatch_gather",
    )(idx, src)


def _expert_kernel(be_ref, x_ref, wg_ref, wu_ref, wd_ref, o_ref):
    xb = x_ref[...].astype(BF16)
    a = jnp.dot(xb, wg_ref[...], preferred_element_type=F32)
    u = jnp.dot(xb, wu_ref[...], preferred_element_type=F32)
    o_ref[...] = jnp.dot((_silu(a) * u).astype(BF16), wd_ref[...], preferred_element_type=F32)


def expert_mlp(block_e, x_rows, w_gate, w_up, w_down, tm):
    n_rows = x_rows.shape[0]
    return pl.pallas_call(
        _expert_kernel,
        grid_spec=pltpu.PrefetchScalarGridSpec(
            num_scalar_prefetch=1,
            grid=(n_rows // tm,),
            in_specs=[
                pl.BlockSpec((tm, D_MODEL), lambda i, be: (i, 0)),
                pl.BlockSpec((None, D_MODEL, EXPERT_FF), lambda i, be: (be[i], 0, 0)),
                pl.BlockSpec((None, D_MODEL, EXPERT_FF), lambda i, be: (be[i], 0, 0)),
                pl.BlockSpec((None, EXPERT_FF, D_MODEL), lambda i, be: (be[i], 0, 0)),
            ],
            out_specs=pl.BlockSpec((tm, D_MODEL), lambda i, be: (i, 0)),
        ),
        out_shape=jax.ShapeDtypeStruct((n_rows, D_MODEL), F32),
        compiler_params=_params("arbitrary"),
        name="expert_mlp",
    )(block_e, x_rows, w_gate, w_up, w_down)


def _combine_kernel(dest_ref, x_ref, gates_ref, mod_ref, y_ref, o_ref, buf, sem, *, tm, n_tok):
    base = pl.program_id(0) * tm

    def copy(k, r):
        return pltpu.make_async_copy(y_ref.at[pl.ds(dest_ref[k * n_tok + base + r], 1)],
                                     buf.at[k, pl.ds(r, 1)], sem)

    def issue(r, carry):
        copy(0, r).start()
        copy(1, r).start()
        return carry

    def drain(r, carry):
        copy(0, r).wait()
        copy(1, r).wait()
        return carry

    lax.fori_loop(0, tm, issue, 0)
    lax.fori_loop(0, tm, drain, 0)
    moe = gates_ref[:, 0:1] * buf[0] + gates_ref[:, 1:2] * buf[1]
    o_ref[...] = x_ref[...] + mod_ref[5:6, :] * moe


def moe_combine(x, y_rows, dest, gates, mod, mod_row, tm=256):
    n_tok = x.shape[0]
    return pl.pallas_call(
        functools.partial(_combine_kernel, tm=tm, n_tok=n_tok),
        grid_spec=pltpu.PrefetchScalarGridSpec(
            num_scalar_prefetch=1,
            grid=(n_tok // tm,),
            in_specs=[
                pl.BlockSpec((tm, D_MODEL), lambda i, d: (i, 0)),
                pl.BlockSpec((tm, TOP_K), lambda i, d: (i, 0)),
                pl.BlockSpec((None, 6, D_MODEL), lambda i, d: (mod_row(tm)(i), 0, 0)),
                pl.BlockSpec(memory_space=pl.ANY),
            ],
            out_specs=pl.BlockSpec((tm, D_MODEL), lambda i, d: (i, 0)),
            scratch_shapes=[pltpu.VMEM((TOP_K, tm, D_MODEL), F32), pltpu.SemaphoreType.DMA(())],
        ),
        out_shape=jax.ShapeDtypeStruct((n_tok, D_MODEL), F32),
        compiler_params=_params("arbitrary"),
        name="moe_combine",
    )(dest, x, gates, mod, y_rows)


def dispatch_plan(route, tm):
    n_tok = route.shape[1]
    e_idx = route[0:TOP_K].astype(jnp.int32)
    flat_e = e_idx.reshape(-1)
    n_assign = flat_e.shape[0]
    onehot = (flat_e[:, None] == jnp.arange(N_EXPERTS, dtype=jnp.int32)[None, :]).astype(jnp.int32)
    csum = jnp.cumsum(onehot, axis=0)
    rank = jnp.sum(onehot * csum, axis=1) - 1
    counts = csum[-1]
    padded = (counts + tm - 1) // tm * tm
    padded_end = jnp.cumsum(padded)
    padded_start = padded_end - padded
    dest = padded_start[flat_e] + rank
    n_blocks = n_assign // tm + N_EXPERTS
    n_rows = n_blocks * tm
    tok = jnp.tile(jnp.arange(n_tok, dtype=jnp.int32), TOP_K)
    row_tok = jnp.zeros((n_rows,), jnp.int32).at[dest].set(tok)
    block_e = jnp.minimum(
        jnp.searchsorted(padded_end, jnp.arange(n_blocks, dtype=jnp.int32) * tm, side='right'),
        N_EXPERTS - 1).astype(jnp.int32)
    gates = route[TOP_K:2 * TOP_K].T
    return row_tok, block_e, dest, gates


def _lane_vec(pairs):
    v = jnp.zeros((LANES,), F32)
    for off, arr in pairs:
        v = lax.dynamic_update_slice(v, arr.reshape(-1).astype(F32), (off,))
    return v.reshape(1, LANES)


def _rope_tables(n_tok):
    n_rows = n_tok // GRID_W
    rows = np.repeat(np.arange(n_rows), GRID_W).astype(np.float32)
    cols = np.tile(np.arange(GRID_W), n_rows).astype(np.float32)
    inv = 1.0 / (ROPE_THETA ** (jnp.arange(ROPE_AXIS_PAIRS, dtype=F32) / ROPE_AXIS_PAIRS))
    ang = jnp.concatenate([rows[:, None] * inv, cols[:, None] * inv], axis=-1)
    cos, sin = jnp.cos(ang), jnp.sin(ang)
    return jnp.concatenate([cos, cos], axis=-1), jnp.concatenate([-sin, sin], axis=-1)


def _reorder_w_in(w):
    c0 = 4 * GDN_WIDTH
    c1 = c0 + 2 * N_DIR * GDN_HEADS
    c2 = w.shape[1] - N_DIR * SSD_HEADS
    main = jnp.concatenate([w[:, :c0], w[:, c1:c2]], axis=1)
    small = jnp.concatenate([w[:, c0:c1], w[:, c2:],
                             jnp.zeros((w.shape[0], LANES - SM_USED), w.dtype)], axis=1)
    return main.astype(BF16), small.astype(BF16)


def _group_layer(x, mod, mod_row, lp, shared, n_seq, seq_len, tm_in, tq, moe_tm, ctx):
    proj, small = in_projection(x, mod, mod_row, lp['norm_mix'], lp['w_in_main'], lp['w_in_small'], tm_in)
    if ctx is None:
        att, k_new, v_new = attention(proj, lp['q_norm'], lp['k_norm'], n_seq, seq_len, tq)
        gdn, s_gdn = gated_deltanet(proj, small, lp['gdn_conv'], lp['gdn_alog_vec'], lp['gdn_dtb_vec'],
                                    lp['gdn_norm'], n_seq, seq_len)
        ssd, s_ssd = ssd_mixer(proj, small, lp['ssd_conv_w'], lp['ssd_conv_b'], lp['ssd_alog_vec'],
                               lp['ssd_dtb_vec'], lp['ssd_d_vec'], lp['ssd_norm'], n_seq, seq_len)
        new_ctx = (k_new, v_new, s_gdn, s_ssd)
    else:
        cache_k4, cache_v4, state_gdn, state_ssd, layer, cos2, sin2 = ctx
        att = attention(proj, lp['q_norm'], lp['k_norm'], n_seq, seq_len, tq,
                        ctx=(cache_k4, cache_v4, layer, cos2, sin2))[0]
        gdn, _ = gated_deltanet(proj, small, lp['gdn_conv'], lp['gdn_alog_vec'], lp['gdn_dtb_vec'],
                                lp['gdn_norm'], n_seq, seq_len, state=(state_gdn, layer))
        ssd, _ = ssd_mixer(proj, small, lp['ssd_conv_w'], lp['ssd_conv_b'], lp['ssd_alog_vec'],
                           lp['ssd_dtb_vec'], lp['ssd_d_vec'], lp['ssd_norm'], n_seq, seq_len,
                           state=(state_ssd, layer))
        new_ctx = None
    x_mid, h2, route = out_projection(x, gdn, att, ssd, mod, mod_row, lp['norm_ffn'], lp['w_out'],
                                      shared['w_router'], shared['b_router'])
    row_tok, block_e, dest, gates = dispatch_plan(route, moe_tm)
    x_rows = gather_rows(h2, row_tok)
    y_rows = expert_mlp(block_e, x_rows, lp['w_gate'], lp['w_up'], lp['w_down'], moe_tm)
    x_out = moe_combine(x_mid, y_rows, dest, gates, mod, mod_row)
    return x_out, new_ctx


def kernel(x_prompt, x_sample, c, cache_k, cache_v, state_gdn, state_ssd, c_ctx, w_ada, b_ada, norm_mix,
           norm_ffn, w_in, gdn_conv, gdn_a_log, gdn_dt_bias, gdn_norm, q_norm, k_norm, ssd_conv_w,
           ssd_conv_b, ssd_a_log, ssd_dt_bias, ssd_d, ssd_norm, w_out, w_router, b_router, w_gate, w_up,
           w_down):
    n_ctx, ctx_len, _ = x_prompt.shape
    n_lat, lat_len, _ = x_sample.shape
    past = cache_k.shape[2]
    kv_w = ATT_KV_HEADS * HEAD_DIM

    cond = jnp.concatenate([c_ctx[None, :], c, jnp.zeros((SUBLANES - 1 - n_lat, D_MODEL), F32)], axis=0)
    mod_all = ada_modulation(cond, w_ada, b_ada).reshape(DEPTH, SUBLANES, 6, D_MODEL)

    shared = {
        'w_router': jnp.concatenate([w_router.T, jnp.zeros((LANES - N_EXPERTS, D_MODEL), F32)],
                                    axis=0).astype(BF16),
        'b_router': jnp.concatenate([b_router, jnp.zeros((LANES - N_EXPERTS,), F32)]).reshape(LANES, 1),
    }
    cos2, sin2 = _rope_tables(lat_len)
    cache_k4 = cache_k.reshape(n_lat, DEPTH, past, kv_w)
    cache_v4 = cache_v.reshape(n_lat, DEPTH, past, kv_w)

    tm_ctx = 1024
    tm_lat = 1024
    lat_blocks_per_seq_in = lat_len // tm_lat

    y_ctx = x_prompt.reshape(n_ctx * ctx_len, D_MODEL)
    y_lat = x_sample.reshape(n_lat * lat_len, D_MODEL)
    new_k, new_v, new_gdn, new_ssd = [], [], [], []
    for l in range(DEPTH):
        w_main, w_small = _reorder_w_in(w_in[l])
        lp = {
            'norm_mix': norm_mix[l].reshape(1, D_MODEL), 'norm_ffn': norm_ffn[l].reshape(1, D_MODEL),
            'w_in_main': w_main, 'w_in_small': w_small,
            'gdn_conv': gdn_conv[l],
            'gdn_alog_vec': _lane_vec([(SM_DECAY, gdn_a_log[l])]),
            'gdn_dtb_vec': _lane_vec([(SM_DECAY, gdn_dt_bias[l])]),
            'gdn_norm': gdn_norm[l], 'q_norm': q_norm[l], 'k_norm': k_norm[l],
            'ssd_conv_w': ssd_conv_w[l], 'ssd_conv_b': ssd_conv_b[l],
            'ssd_alog_vec': _lane_vec([(SM_DT, ssd_a_log[l])]),
            'ssd_dtb_vec': _lane_vec([(SM_DT, ssd_dt_bias[l])]),
            'ssd_d_vec': jnp.repeat(ssd_d[l], SSD_HEAD_DIM).reshape(1, SSD_WIDTH),
            'ssd_norm': ssd_norm[l],
            'w_out': w_out[l].astype(BF16),
            'w_gate': w_gate[l].astype(BF16), 'w_up': w_up[l].astype(BF16), 'w_down': w_down[l].astype(BF16),
        }
        mod = mod_all[l]
        y_ctx, (k_l, v_l, g_l, s_l) = _group_layer(
            y_ctx, mod, lambda tm: (lambda i: 0), lp, shared, n_ctx, ctx_len, tm_ctx, ctx_len, 256, None)
        new_k.append(k_l.reshape(n_ctx, ctx_len, ATT_KV_HEADS, HEAD_DIM))
        new_v.append(v_l.reshape(n_ctx, ctx_len, ATT_KV_HEADS, HEAD_DIM))
        new_gdn.append(g_l)
        new_ssd.append(s_l)
        ctx = (cache_k4, cache_v4, state_gdn, state_ssd, l, cos2, sin2)
        y_lat, _ = _group_layer(
            y_lat, mod, lambda tm: (lambda i: 1 + i // (lat_len // tm)), lp, shared, n_lat, lat_len,
            tm_lat, 256, 256, ctx)
    return (y_ctx.reshape(n_ctx, ctx_len, D_MODEL), y_lat.reshape(n_lat, lat_len, D_MODEL),
            jnp.stack(new_k, axis=1), jnp.stack(new_v, axis=1),
            jnp.stack(new_gdn, axis=1), jnp.stack(new_ssd, axis=1))
```

```python
import functools

import numpy as np
import jax
import jax.numpy as jnp
from jax import lax
from jax.experimental import pallas as pl
from jax.experimental.pallas import tpu as pltpu

F32 = jnp.float32
BF16 = jnp.bfloat16

D_MODEL = 2048
DEPTH = 2
GRID_W = 64
N_DIR = 2
EPS = 1e-6
CONV_TAPS = 5
GDN_HEADS = 4
GDN_DK = 128
GDN_DV = 128
GDN_WIDTH = GDN_HEADS * GDN_DV
ATT_HEADS = 8
ATT_KV_HEADS = 2
HEAD_DIM = 128
ATT_GROUP = ATT_HEADS // ATT_KV_HEADS
ATT_WIDTH = ATT_HEADS * HEAD_DIM
ROPE_THETA = 10000.0
ROPE_AXIS_PAIRS = HEAD_DIM // 4
SSD_HEADS = 8
SSD_HEAD_DIM = 64
SSD_WIDTH = SSD_HEADS * SSD_HEAD_DIM
SSD_GROUPS = 2
SSD_HEADS_PER_GROUP = SSD_HEADS // SSD_GROUPS
SSD_GROUP_WIDTH = SSD_WIDTH // SSD_GROUPS
SSD_STATE = 128
MIX_WIDTH = GDN_WIDTH + ATT_WIDTH + SSD_WIDTH
N_EXPERTS = 16
N_EXPERT_GROUPS = 4
EXPERTS_PER_GROUP = N_EXPERTS // N_EXPERT_GROUPS
TOP_K = 2
EXPERT_FF = 1024

LANES = 128
SUBLANES = 8
CHUNK = 128
GDN_PREP_CHUNKS = 8
VMEM_LIMIT = 56 * 1024 * 1024

OFF_GQKV = 0
OFF_GGATE = 3 * GDN_WIDTH
OFF_AQ = OFF_GGATE + GDN_WIDTH
OFF_AK = OFF_AQ + ATT_WIDTH
OFF_AV = OFF_AK + ATT_KV_HEADS * HEAD_DIM
OFF_SZ = OFF_AV + ATT_KV_HEADS * HEAD_DIM
OFF_SXBC = OFF_SZ + SSD_WIDTH
MAIN_COLS = OFF_SXBC + SSD_WIDTH + 2 * SSD_GROUPS * SSD_STATE
SM_BETA = 0
SM_DECAY = SM_BETA + N_DIR * GDN_HEADS
SM_DT = SM_DECAY + N_DIR * GDN_HEADS
SM_USED = SM_DT + N_DIR * SSD_HEADS


def _silu(x):
    return x * jax.nn.sigmoid(x)


def _softplus(x):
    return jnp.maximum(x, 0.0) + jnp.log1p(jnp.exp(-jnp.abs(x)))


def _dot(a, b):
    return jnp.dot(a.astype(BF16), b.astype(BF16), preferred_element_type=F32)


def _dot_nt(a, b):
    return lax.dot_general(a.astype(BF16), b.astype(BF16), (((1,), (1,)), ((), ())),
                           preferred_element_type=F32)


def _dot_tn(a, b):
    return lax.dot_general(a.astype(BF16), b.astype(BF16), (((0,), (0,)), ((), ())),
                           preferred_element_type=F32)


def _split3(x):
    hi = x.astype(BF16)
    r1 = x - hi.astype(F32)
    mid = r1.astype(BF16)
    lo = (r1 - mid.astype(F32)).astype(BF16)
    return hi, mid, lo


def _params(*sem):
    return pltpu.CompilerParams(dimension_semantics=sem, vmem_limit_bytes=VMEM_LIMIT)


def _mod_kernel(c_ref, w_ref, b_ref, o_ref):
    s = _silu(c_ref[...])
    o_ref[...] = _dot(s, w_ref[...]) + b_ref[...]


def ada_modulation(cond, w_ada, b_ada, tn=1024):
    n_out = 6 * D_MODEL
    return pl.pallas_call(
        _mod_kernel,
        grid=(DEPTH, n_out // tn),
        in_specs=[
            pl.BlockSpec((SUBLANES, D_MODEL), lambda l, j: (0, 0)),
            pl.BlockSpec((None, D_MODEL, tn), lambda l, j: (l, 0, j)),
            pl.BlockSpec((None, 1, tn), lambda l, j: (l, 0, j)),
        ],
        out_specs=pl.BlockSpec((None, SUBLANES, tn), lambda l, j: (l, 0, j)),
        out_shape=jax.ShapeDtypeStruct((DEPTH, SUBLANES, n_out), F32),
        compiler_params=_params("parallel", "parallel"),
        name="ada_modulation",
    )(cond, w_ada, b_ada.reshape(DEPTH, 1, n_out))


def _inproj_kernel(x_ref, mod_ref, g_ref, w_ref, ws_ref, o_ref, os_ref, h_scr):
    @pl.when(pl.program_id(1) == 0)
    def _():
        x = x_ref[...]
        y = x * lax.rsqrt(jnp.mean(x * x, axis=-1, keepdims=True) + EPS)
        h = (y * g_ref[...]) * (1.0 + mod_ref[1:2, :]) + mod_ref[0:1, :]
        hb = h.astype(BF16)
        h_scr[...] = hb
        os_ref[...] = jnp.dot(hb, ws_ref[...], preferred_element_type=F32)

    o_ref[...] = jnp.dot(h_scr[...], w_ref[...], preferred_element_type=F32)


def in_projection(x, mod, mod_row, g, w_main, w_small, tm, tn=1280):
    n_tok = x.shape[0]
    return pl.pallas_call(
        _inproj_kernel,
        grid=(n_tok // tm, MAIN_COLS // tn),
        in_specs=[
            pl.BlockSpec((tm, D_MODEL), lambda i, j: (i, 0)),
            pl.BlockSpec((None, 6, D_MODEL), lambda i, j: (mod_row(tm)(i), 0, 0)),
            pl.BlockSpec((1, D_MODEL), lambda i, j: (0, 0)),
            pl.BlockSpec((D_MODEL, tn), lambda i, j: (0, j)),
            pl.BlockSpec((D_MODEL, LANES), lambda i, j: (0, 0)),
        ],
        out_specs=[
            pl.BlockSpec((tm, tn), lambda i, j: (i, j)),
            pl.BlockSpec((tm, LANES), lambda i, j: (i, 0)),
        ],
        out_shape=[jax.ShapeDtypeStruct((n_tok, MAIN_COLS), F32),
                   jax.ShapeDtypeStruct((n_tok, LANES), F32)],
        scratch_shapes=[pltpu.VMEM((tm, D_MODEL), BF16)],
        compiler_params=_params("parallel", "arbitrary"),
        name="in_projection",
    )(x, mod, g, w_main, w_small)


def _rms_head(x, g):
    return x * lax.rsqrt(jnp.mean(x * x, axis=-1, keepdims=True) + EPS) * g


def _rope(x, cos2, sin2):
    return x * cos2 + pltpu.roll(x, HEAD_DIM // 2, 1) * sin2


def _attn_kernel(*refs, has_ctx):
    if has_ctx:
        (q_ref, k_ref, v_ref, qn_ref, kn_ref, cq_ref, sq_ref, ck_ref, sk_ref, xk_ref, xv_ref,
         o_ref, k_scr, v_scr) = refs
    else:
        (q_ref, k_ref, v_ref, qn_ref, kn_ref, o_ref, ko_ref, vo_ref, k_scr, v_scr) = refs

    @pl.when(pl.program_id(2) == 0)
    def _():
        kn = _rms_head(k_ref[...], kn_ref[...])
        v = v_ref[...]
        if has_ctx:
            kn = _rope(kn, ck_ref[...], sk_ref[...])
        else:
            ko_ref[...] = kn
            vo_ref[...] = v
        k_scr[...] = kn.astype(BF16)
        v_scr[...] = v.astype(BF16)

    kb = k_scr[...]
    vb = v_scr[...]
    if has_ctx:
        xk = xk_ref[...].astype(BF16)
        xv = xv_ref[...].astype(BF16)
    scale = HEAD_DIM ** -0.5
    for h in range(ATT_GROUP):
        q = _rms_head(q_ref[:, h * HEAD_DIM:(h + 1) * HEAD_DIM], qn_ref[...])
        if has_ctx:
            q = _rope(q, cq_ref[...], sq_ref[...])
        qb = (q * scale).astype(BF16)
        s1 = _dot_nt(qb, kb)
        m = jnp.max(s1, axis=-1, keepdims=True)
        if has_ctx:
            s2 = _dot_nt(qb, xk)
            m = jnp.maximum(m, jnp.max(s2, axis=-1, keepdims=True))
        p1 = jnp.exp(s1 - m)
        den = jnp.sum(p1, axis=-1, keepdims=True)
        acc = jnp.dot(p1.astype(BF16), vb, preferred_element_type=F32)
        if has_ctx:
            p2 = jnp.exp(s2 - m)
            den = den + jnp.sum(p2, axis=-1, keepdims=True)
            acc = acc + jnp.dot(p2.astype(BF16), xv, preferred_element_type=F32)
        o_ref[:, h * HEAD_DIM:(h + 1) * HEAD_DIM] = (acc / den).astype(o_ref.dtype)


def attention(proj, q_norm, k_norm, n_seq, seq_len, tq, ctx=None):
    n_tok = n_seq * seq_len
    nq = seq_len // tq
    gw = ATT_GROUP * HEAD_DIM
    has_ctx = ctx is not None
    in_specs = [
        pl.BlockSpec((tq, gw), lambda b, g, i: (b * nq + i, OFF_AQ // gw + g)),
        pl.BlockSpec((seq_len, HEAD_DIM), lambda b, g, i: (b, OFF_AK // HEAD_DIM + g)),
        pl.BlockSpec((seq_len, HEAD_DIM), lambda b, g, i: (b, OFF_AV // HEAD_DIM + g)),
        pl.BlockSpec((1, HEAD_DIM), lambda b, g, i: (0, 0)),
        pl.BlockSpec((1, HEAD_DIM), lambda b, g, i: (0, 0)),
    ]
    args = [proj, proj, proj, q_norm.reshape(1, HEAD_DIM), k_norm.reshape(1, HEAD_DIM)]
    out_specs = [pl.BlockSpec((tq, gw), lambda b, g, i: (b * nq + i, g))]
    out_shape = [jax.ShapeDtypeStruct((n_tok, ATT_WIDTH), BF16)]
    if has_ctx:
        cache_k4, cache_v4, layer, cos2, sin2 = ctx
        past = cache_k4.shape[2]
        in_specs += [
            pl.BlockSpec((tq, HEAD_DIM), lambda b, g, i: (i, 0)),
            pl.BlockSpec((tq, HEAD_DIM), lambda b, g, i: (i, 0)),
            pl.BlockSpec((seq_len, HEAD_DIM), lambda b, g, i: (0, 0)),
            pl.BlockSpec((seq_len, HEAD_DIM), lambda b, g, i: (0, 0)),
            pl.BlockSpec((None, None, past, HEAD_DIM), lambda b, g, i: (b, layer, 0, g)),
            pl.BlockSpec((None, None, past, HEAD_DIM), lambda b, g, i: (b, layer, 0, g)),
        ]
        args += [cos2, sin2, cos2, sin2, cache_k4, cache_v4]
    else:
        kv_w = ATT_KV_HEADS * HEAD_DIM
        out_specs += [pl.BlockSpec((seq_len, HEAD_DIM), lambda b, g, i: (b, g)),
                      pl.BlockSpec((seq_len, HEAD_DIM), lambda b, g, i: (b, g))]
        out_shape += [jax.ShapeDtypeStruct((n_tok, kv_w), F32),
                      jax.ShapeDtypeStruct((n_tok, kv_w), F32)]
    return pl.pallas_call(
        functools.partial(_attn_kernel, has_ctx=has_ctx),
        grid=(n_seq, ATT_KV_HEADS, nq),
        in_specs=in_specs,
        out_specs=out_specs,
        out_shape=out_shape,
        scratch_shapes=[pltpu.VMEM((seq_len, HEAD_DIM), BF16), pltpu.VMEM((seq_len, HEAD_DIM), BF16)],
        compiler_params=_params("parallel", "parallel", "arbitrary"),
        name="attention_latent" if has_ctx else "attention_context",
    )(*args)


def _iota2(shape, dim):
    return lax.broadcasted_iota(jnp.int32, shape, dim)


def _conv_chunk(x_ref, w, c, n_chunks, bias=None):
    seq_len = n_chunks * CHUNK
    r0 = pl.multiple_of(c * CHUNK, CHUNK)
    cur = x_ref[pl.ds(r0, CHUNK), :]
    lo = pl.multiple_of(jnp.maximum(r0 - SUBLANES, 0), SUBLANES)
    hi = pl.multiple_of(jnp.minimum(r0 + CHUNK, seq_len - SUBLANES), SUBLANES)
    prev = jnp.where(c > 0, x_ref[pl.ds(lo, SUBLANES), :], 0.0)
    nxt = jnp.where(c < n_chunks - 1, x_ref[pl.ds(hi, SUBLANES), :], 0.0)
    ext = jnp.concatenate([prev, cur, nxt], axis=0)
    n_ext = CHUNK + 2 * SUBLANES
    pad = CONV_TAPS // 2
    acc = None
    for j in range(CONV_TAPS):
        shifted = ext if j == pad else pltpu.roll(ext, (pad - j) % n_ext, 0)
        term = shifted[SUBLANES:SUBLANES + CHUNK, :] * w[j:j + 1, :]
        acc = term if acc is None else acc + term
    if bias is not None:
        acc = acc + bias
    return acc


def _chunk_cumsums(a):
    tril = (_iota2((CHUNK, CHUNK), 1) <= _iota2((CHUNK, CHUNK), 0)).astype(BF16)
    hi, mid, lo = _split3(a)
    pre = (jnp.dot(tril, hi, preferred_element_type=F32)
           + jnp.dot(tril, mid, preferred_element_type=F32)
           + jnp.dot(tril, lo, preferred_element_type=F32))
    tot = pre[CHUNK - 1:CHUNK, :]
    suf = tot - pre + a
    return pre, suf, tot


def _pick_col(x, lane_ids, c):
    return jnp.sum(jnp.where(lane_ids == c, x, 0.0), axis=1, keepdims=True)


def _pick_row(x, row_ids, r):
    return jnp.sum(jnp.where(row_ids == r, x, 0.0), axis=0, keepdims=True)


def _scan_order_cumsum(a_col, a_row, lower):
    ii = _iota2((CHUNK, CHUNK), 0)
    jj = _iota2((CHUNK, CHUNK), 1)
    tril = (jj <= ii).astype(BF16)
    triu = (jj >= ii).astype(BF16)
    left, right = (tril, triu) if lower else (triu, tril)
    g_col = sum(jnp.dot(left, p, preferred_element_type=F32)
                for p in _split3(jnp.broadcast_to(a_col, (CHUNK, CHUNK))))
    g_row = sum(jnp.dot(p, right, preferred_element_type=F32)
                for p in _split3(jnp.broadcast_to(a_row, (CHUNK, CHUNK))))
    tot = g_col[CHUNK - 1:CHUNK, :] if lower else g_col[0:1, :]
    return g_col, g_row, tot


def _decay_matrix(g_col, g_row, incl):
    d = g_col - g_row
    return jnp.where(incl, jnp.exp(jnp.where(incl, d, 0.0)), 0.0)


def _unit_tri_inverses(lms, lowers):
    ii = _iota2((CHUNK, CHUNK), 0)
    jj = _iota2((CHUNK, CHUNK), 1)
    xs = None
    s = 1
    while s < CHUNK:
        sh = s.bit_length() - 1
        bi = ii >> sh
        bj = jj >> sh
        same = (bi >> 1) == (bj >> 1)
        m_lower = same & ((bi & 1) == 1) & ((bj & 1) == 0)
        m_upper = same & ((bi & 1) == 0) & ((bj & 1) == 1)
        offs = [jnp.where(m_lower if lower else m_upper, lm, 0.0) for lm, lower in zip(lms, lowers)]
        if xs is None:
            eye = (ii == jj).astype(F32)
            xs = [eye - off for off in offs]
        else:
            ps = [_dot(off, x) for off, x in zip(offs, xs)]
            xs = [x - _dot(x, p) for x, p in zip(xs, ps)]
        s *= 2
    return xs


def _gdn_kernel(*refs, n_chunks, has_state):
    (q_ref, k_ref, v_ref, gate_ref, sm_ref, wq_ref, wk_ref, wv_ref, alog_ref, dtb_ref, gn_ref) = refs[:11]
    pos = 11
    if has_state:
        s0_ref = refs[pos]
        pos += 1
    out_ref, sfin_ref = refs[pos:pos + 2]
    (u_scr, w_scr, qk_scr, qg_scr, kg_scr, egl_scr, o_scr) = refs[pos + 2:]
    h = pl.program_id(1)
    lane_ids = _iota2((CHUNK, LANES), 1)
    ii = _iota2((CHUNK, CHUNK), 0)
    jj = _iota2((CHUNK, CHUNK), 1)
    neg_a = -jnp.exp(alog_ref[...])
    dtb = dtb_ref[...]
    prep_chunks = int(np.gcd(n_chunks, GDN_PREP_CHUNKS))

    def prep(i, carry):
        chains = []
        for cc in range(prep_chunks):
            c = i * prep_chunks + cc
            r0 = pl.multiple_of(c * CHUNK, CHUNK)
            q = _silu(_conv_chunk(q_ref, wq_ref[...], c, n_chunks))
            k = _silu(_conv_chunk(k_ref, wk_ref[...], c, n_chunks))
            v = _silu(_conv_chunk(v_ref, wv_ref[...], c, n_chunks))
            q = q * lax.rsqrt(jnp.sum(q * q, axis=-1, keepdims=True) + EPS) * (GDN_DK ** -0.5)
            k = k * lax.rsqrt(jnp.sum(k * k, axis=-1, keepdims=True) + EPS)
            sm = sm_ref[pl.ds(r0, CHUNK), :]
            beta_all = jax.nn.sigmoid(sm)
            la_all = neg_a * _softplus(sm + dtb)
            la_t = la_all.T
            kk = _dot_nt(k, k)
            qk = _dot_nt(q, k)
            for d in range(N_DIR):
                lower = d == 0
                col = SM_DECAY + d * GDN_HEADS + h
                g_col, g_row, g_last = _scan_order_cumsum(
                    _pick_col(la_all, lane_ids, col), _pick_row(la_t, ii, col), lower)
                beta = _pick_col(beta_all, lane_ids, SM_BETA + d * GDN_HEADS + h)
                incl = (jj <= ii) if lower else (jj >= ii)
                strict = (jj < ii) if lower else (jj > ii)
                decay = _decay_matrix(g_col, g_row, incl)
                chains.append(dict(c=c, r0=r0, d=d, q=q, k=k, v=v, qk=qk, beta=beta, decay=decay,
                                   g_col=g_col, g_last=g_last,
                                   lm=jnp.where(strict, kk * beta * decay, 0.0)))
        ts = _unit_tri_inverses([ch['lm'] for ch in chains], [ch['d'] == 0 for ch in chains])
        egs = [jnp.exp(ch['g_col']) for ch in chains]
        us = [_dot(t, ch['v'] * ch['beta']) for t, ch in zip(ts, chains)]
        ws = [_dot(t, ch['k'] * (ch['beta'] * eg)) for t, ch, eg in zip(ts, chains, egs)]
        for ch, eg, u, w in zip(chains, egs, us, ws):
            d, rows = ch['d'], pl.ds(ch['r0'], CHUNK)
            u_scr[d, rows, :] = u
            w_scr[d, rows, :] = w.astype(BF16)
            qk_scr[d, rows, :] = (ch['qk'] * ch['decay']).astype(BF16)
            qg_scr[d, rows, :] = (ch['q'] * eg).astype(BF16)
            kg_scr[d, rows, :] = (ch['k'] * jnp.exp(ch['g_last'] - ch['g_col'])).astype(BF16)
            egl_scr[d, pl.ds(ch['c'], 1), :] = jnp.exp(ch['g_last'])
        return carry

    lax.fori_loop(0, n_chunks // prep_chunks, prep, 0)
    o_scr[...] = jnp.zeros_like(o_scr)

    def scan(i, carry):
        cs = (i, n_chunks - 1 - i)
        rows = [pl.ds(pl.multiple_of(c * CHUNK, CHUNK), CHUNK) for c in cs]
        sbs = [s.astype(BF16) for s in carry]
        ws = [jnp.dot(w_scr[d, rows[d], :], sbs[d], preferred_element_type=F32) for d in range(N_DIR)]
        gs = [jnp.dot(qg_scr[d, rows[d], :], sbs[d], preferred_element_type=F32) for d in range(N_DIR)]
        vbs = [(u_scr[d, rows[d], :] - ws[d]).astype(BF16) for d in range(N_DIR)]
        os = [gs[d] + jnp.dot(qk_scr[d, rows[d], :], vbs[d], preferred_element_type=F32)
              for d in range(N_DIR)]
        new = [carry[d] * egl_scr[d, pl.ds(cs[d], 1), :] + _dot_tn(kg_scr[d, rows[d], :], vbs[d])
               for d in range(N_DIR)]
        for d in range(N_DIR):
            o_scr[rows[d], :] += os[d]
        return tuple(new)

    if has_state:
        init = (s0_ref[0], s0_ref[1])
    else:
        init = (jnp.zeros((GDN_DK, GDN_DV), F32), jnp.zeros((GDN_DK, GDN_DV), F32))
    s_f, s_b = lax.fori_loop(0, n_chunks, scan, init)
    sfin_ref[0] = s_f
    sfin_ref[1] = s_b

    def finish(c, carry):
        r0 = pl.multiple_of(c * CHUNK, CHUNK)
        o = o_scr[pl.ds(r0, CHUNK), :]
        y = o * lax.rsqrt(jnp.mean(o * o, axis=-1, keepdims=True) + EPS) * gn_ref[...]
        out_ref[pl.ds(r0, CHUNK), :] = (y * _silu(gate_ref[pl.ds(r0, CHUNK), :])).astype(out_ref.dtype)
        return carry

    lax.fori_loop(0, n_chunks, finish, 0)


def gated_deltanet(proj, small, conv_w, alog_vec, dtb_vec, gnorm, n_seq, seq_len, state=None):
    n_tok = n_seq * seq_len
    n_chunks = seq_len // CHUNK
    has_state = state is not None
    nh = GDN_HEADS
    in_specs = [
        pl.BlockSpec((seq_len, LANES), lambda b, h: (b, h)),
        pl.BlockSpec((seq_len, LANES), lambda b, h: (b, nh + h)),
        pl.BlockSpec((seq_len, LANES), lambda b, h: (b, 2 * nh + h)),
        pl.BlockSpec((seq_len, LANES), lambda b, h: (b, 3 * nh + h)),
        pl.BlockSpec((seq_len, LANES), lambda b, h: (b, 0)),
        pl.BlockSpec((CONV_TAPS, LANES), lambda b, h: (0, h)),
        pl.BlockSpec((CONV_TAPS, LANES), lambda b, h: (0, nh + h)),
        pl.BlockSpec((CONV_TAPS, LANES), lambda b, h: (0, 2 * nh + h)),
        pl.BlockSpec((1, LANES), lambda b, h: (0, 0)),
        pl.BlockSpec((1, LANES), lambda b, h: (0, 0)),
        pl.BlockSpec((1, LANES), lambda b, h: (0, 0)),
    ]
    args = [proj, proj, proj, proj, small, conv_w, conv_w, conv_w, alog_vec, dtb_vec,
            gnorm.reshape(1, GDN_DV)]
    if has_state:
        state_gdn, layer = state
        in_specs.append(pl.BlockSpec((None, None, N_DIR, None, GDN_DK, GDN_DV),
                                     lambda b, h: (b, layer, 0, h, 0, 0)))
        args.append(state_gdn)
    return pl.pallas_call(
        functools.partial(_gdn_kernel, n_chunks=n_chunks, has_state=has_state),
        grid=(n_seq, GDN_HEADS),
        in_specs=in_specs,
        out_specs=[
            pl.BlockSpec((seq_len, LANES), lambda b, h: (b, h)),
            pl.BlockSpec((None, N_DIR, None, GDN_DK, GDN_DV), lambda b, h: (b, 0, h, 0, 0)),
        ],
        out_shape=[jax.ShapeDtypeStruct((n_tok, GDN_WIDTH), BF16),
                   jax.ShapeDtypeStruct((n_seq, N_DIR, GDN_HEADS, GDN_DK, GDN_DV), F32)],
        scratch_shapes=[
            pltpu.VMEM((N_DIR, seq_len, GDN_DV), F32),
            pltpu.VMEM((N_DIR, seq_len, GDN_DK), BF16),
            pltpu.VMEM((N_DIR, seq_len, CHUNK), BF16),
            pltpu.VMEM((N_DIR, seq_len, GDN_DK), BF16),
            pltpu.VMEM((N_DIR, seq_len, GDN_DK), BF16),
            pltpu.VMEM((N_DIR, max(n_chunks, SUBLANES), LANES), F32),
            pltpu.VMEM((seq_len, GDN_DV), F32),
        ],
        compiler_params=_params("parallel", "parallel"),
        name="gdn_latent" if has_state else "gdn_context",
    )(*args)


def _ssd_kernel(*refs, n_chunks, has_state):
    (x_ref, b_ref, c_ref, z_ref, sm_ref, wx_ref, wb_ref, wc_ref, bx_ref, bb_ref, bc_ref,
     alog_ref, dtb_ref, dvec_ref, gn_ref) = refs[:15]
    pos = 15
    if has_state:
        h0_ref = refs[pos]
        pos += 1
    out_ref, hfin_ref = refs[pos:pos + 2]
    xs_scr, bs_scr, cs_scr, y_scr, h_scr, act_scr = refs[pos + 2:]
    g = pl.program_id(1)
    hpg = SSD_HEADS_PER_GROUP
    P = SSD_HEAD_DIM
    lane_ids = _iota2((CHUNK, LANES), 1)
    lane_row = _iota2((1, LANES), 1)
    ii = _iota2((CHUNK, CHUNK), 0)
    jj = _iota2((CHUNK, CHUNK), 1)
    neg_a = -jnp.exp(alog_ref[...])
    dtb = dtb_ref[...]

    def prep(c, carry):
        r0 = pl.multiple_of(c * CHUNK, CHUNK)
        xs_scr[pl.ds(r0, CHUNK), :] = _silu(_conv_chunk(x_ref, wx_ref[...], c, n_chunks, bx_ref[...]))
        bs_scr[pl.ds(r0, CHUNK), :] = _silu(_conv_chunk(b_ref, wb_ref[...], c, n_chunks, bb_ref[...]))
        cs_scr[pl.ds(r0, CHUNK), :] = _silu(_conv_chunk(c_ref, wc_ref[...], c, n_chunks, bc_ref[...]))
        return carry

    lax.fori_loop(0, n_chunks, prep, 0)
    y_scr[...] = jnp.zeros_like(y_scr)
    if has_state:
        h_scr[...] = h0_ref[...]
    else:
        h_scr[...] = jnp.zeros_like(h_scr)

    def chunk_dir(d, c):
        r0 = pl.multiple_of(c * CHUNK, CHUNK)
        lower = d == 0
        x = xs_scr[pl.ds(r0, CHUNK), :]
        bm = bs_scr[pl.ds(r0, CHUNK), :]
        cm = cs_scr[pl.ds(r0, CHUNK), :]
        sm = sm_ref[pl.ds(r0, CHUNK), :]
        dt_all = _softplus(sm + dtb)
        a_all = dt_all * neg_a
        pre, suf, tot = _chunk_cumsums(a_all)
        ac = pre if lower else suf
        act_scr[d] = ac.T
        cb = _dot_nt(cm, bm)
        incl = (jj <= ii) if lower else (jj >= ii)
        for hh in range(hpg):
            col = SM_DT + d * SSD_HEADS + g * hpg + hh
            dt = _pick_col(dt_all, lane_ids, col)
            a_col = _pick_col(ac, lane_ids, col)
            a_row = act_scr[d, pl.ds(col, 1), :]
            a_last = jnp.sum(jnp.where(lane_row == col, tot, 0.0), axis=1, keepdims=True)
            xdt = x[:, hh * P:(hh + 1) * P] * dt
            decay = _decay_matrix(a_col, a_row, incl)
            y = _dot(cb * decay, xdt)
            st = _dot_tn(xdt, bm * jnp.exp(a_last - a_col))
            hprev = h_scr[d, hh]
            y = y + _dot_nt(cm * jnp.exp(a_col), hprev)
            h_scr[d, hh] = hprev * jnp.exp(a_last) + st
            y_scr[pl.ds(r0, CHUNK), hh * P:(hh + 1) * P] += y

    def scan(i, carry):
        chunk_dir(0, i)
        chunk_dir(1, n_chunks - 1 - i)
        return carry

    lax.fori_loop(0, n_chunks, scan, 0)
    hfin_ref[...] = h_scr[...]

    def finish(c, carry):
        r0 = pl.multiple_of(c * CHUNK, CHUNK)
        y = y_scr[pl.ds(r0, CHUNK), :] + dvec_ref[...] * xs_scr[pl.ds(r0, CHUNK), :]
        y = y * _silu(z_ref[pl.ds(r0, CHUNK), :])
        y = y * lax.rsqrt(jnp.mean(y * y, axis=-1, keepdims=True) + EPS) * gn_ref[...]
        out_ref[pl.ds(r0, CHUNK), :] = y.astype(out_ref.dtype)
        return carry

    lax.fori_loop(0, n_chunks, finish, 0)


def ssd_mixer(proj, small, conv_w, conv_b, alog_vec, dtb_vec, d_vec, gnorm, n_seq, seq_len, state=None):
    n_tok = n_seq * seq_len
    n_chunks = seq_len // CHUNK
    has_state = state is not None
    gw = SSD_GROUP_WIDTH
    hpg = SSD_HEADS_PER_GROUP
    xb = OFF_SXBC // gw
    bb = (OFF_SXBC + SSD_WIDTH) // LANES
    cb = bb + SSD_GROUPS
    zb = OFF_SZ // gw
    in_specs = [
        pl.BlockSpec((seq_len, gw), lambda b, g: (b, xb + g)),
        pl.BlockSpec((seq_len, LANES), lambda b, g: (b, bb + g)),
        pl.BlockSpec((seq_len, LANES), lambda b, g: (b, cb + g)),
        pl.BlockSpec((seq_len, gw), lambda b, g: (b, zb + g)),
        pl.BlockSpec((seq_len, LANES), lambda b, g: (b, 0)),
        pl.BlockSpec((CONV_TAPS, gw), lambda b, g: (0, g)),
        pl.BlockSpec((CONV_TAPS, LANES), lambda b, g: (0, SSD_WIDTH // LANES + g)),
        pl.BlockSpec((CONV_TAPS, LANES), lambda b, g: (0, SSD_WIDTH // LANES + SSD_GROUPS + g)),
        pl.BlockSpec((1, gw), lambda b, g: (0, g)),
        pl.BlockSpec((1, LANES), lambda b, g: (0, SSD_WIDTH // LANES + g)),
        pl.BlockSpec((1, LANES), lambda b, g: (0, SSD_WIDTH // LANES + SSD_GROUPS + g)),
        pl.BlockSpec((1, LANES), lambda b, g: (0, 0)),
        pl.BlockSpec((1, LANES), lambda b, g: (0, 0)),
        pl.BlockSpec((1, gw), lambda b, g: (0, g)),
        pl.BlockSpec((1, gw), lambda b, g: (0, g)),
    ]
    conv_b2 = conv_b.reshape(1, -1)
    args = [proj, proj, proj, proj, small, conv_w, conv_w, conv_w, conv_b2, conv_b2, conv_b2,
            alog_vec, dtb_vec, d_vec, gnorm.reshape(1, SSD_WIDTH)]
    if has_state:
        state_ssd, layer = state
        in_specs.append(pl.BlockSpec((None, None, N_DIR, hpg, SSD_HEAD_DIM, SSD_STATE),
                                     lambda b, g: (b, layer, 0, g, 0, 0)))
        args.append(state_ssd)
    return pl.pallas_call(
        functools.partial(_ssd_kernel, n_chunks=n_chunks, has_state=has_state),
        grid=(n_seq, SSD_GROUPS),
        in_specs=in_specs,
        out_specs=[
            pl.BlockSpec((seq_len, gw), lambda b, g: (b, g)),
            pl.BlockSpec((None, N_DIR, hpg, SSD_HEAD_DIM, SSD_STATE), lambda b, g: (b, 0, g, 0, 0)),
        ],
        out_shape=[jax.ShapeDtypeStruct((n_tok, SSD_WIDTH), BF16),
                   jax.ShapeDtypeStruct((n_seq, N_DIR, SSD_HEADS, SSD_HEAD_DIM, SSD_STATE), F32)],
        scratch_shapes=[
            pltpu.VMEM((seq_len, gw), F32),
            pltpu.VMEM((seq_len, SSD_STATE), F32),
            pltpu.VMEM((seq_len, SSD_STATE), F32),
            pltpu.VMEM((seq_len, gw), F32),
            pltpu.VMEM((N_DIR, hpg, SSD_HEAD_DIM, SSD_STATE), F32),
            pltpu.VMEM((N_DIR, LANES, CHUNK), F32),
        ],
        compiler_params=_params("parallel", "parallel"),
        name="ssd_latent" if has_state else "ssd_context",
    )(*args)


def _top2_sum(a, b, c, d):
    hi1, lo1 = jnp.maximum(a, b), jnp.minimum(a, b)
    hi2, lo2 = jnp.maximum(c, d), jnp.minimum(c, d)
    return jnp.maximum(hi1, hi2) + jnp.maximum(jnp.minimum(hi1, hi2), jnp.maximum(lo1, lo2))


def _outproj_kernel(x_ref, gdn_ref, att_ref, ssd_ref, mod_ref, g_ref, w_ref, wr_ref, br_ref,
                    xo_ref, h_ref, route_ref):
    mixed = (jnp.dot(gdn_ref[...], w_ref[0:GDN_WIDTH, :], preferred_element_type=F32)
             + jnp.dot(att_ref[...], w_ref[GDN_WIDTH:GDN_WIDTH + ATT_WIDTH, :], preferred_element_type=F32)
             + jnp.dot(ssd_ref[...], w_ref[GDN_WIDTH + ATT_WIDTH:, :], preferred_element_type=F32))
    x = x_ref[...] + mod_ref[2:3, :] * mixed
    xo_ref[...] = x
    y = x * lax.rsqrt(jnp.mean(x * x, axis=-1, keepdims=True) + EPS)
    h = (y * g_ref[...]) * (1.0 + mod_ref[4:5, :]) + mod_ref[3:4, :]
    h_ref[...] = h
    lt = _dot_nt(wr_ref[...], h)
    sc = [jax.nn.sigmoid(lt[e:e + 1, :]) for e in range(N_EXPERTS)]
    sel = [sc[e] + br_ref[e:e + 1, :] for e in range(N_EXPERTS)]
    epg = EXPERTS_PER_GROUP
    gs = [_top2_sum(*sel[gi * epg:(gi + 1) * epg]) for gi in range(N_EXPERT_GROUPS)]
    best, bg = gs[0], jnp.zeros_like(gs[0], dtype=jnp.int32)
    for gi in range(1, N_EXPERT_GROUPS):
        upd = gs[gi] > best
        bg = jnp.where(upd, gi, bg)
        best = jnp.where(upd, gs[gi], best)
    cand_sel, cand_sc = [], []
    for k in range(epg):
        vs, vc = sel[k], sc[k]
        for gi in range(1, N_EXPERT_GROUPS):
            vs = jnp.where(bg == gi, sel[gi * epg + k], vs)
            vc = jnp.where(bg == gi, sc[gi * epg + k], vc)
        cand_sel.append(vs)
        cand_sc.append(vc)
    m1, i1, g1 = cand_sel[0], jnp.zeros_like(bg), cand_sc[0]
    for k in range(1, epg):
        upd = cand_sel[k] > m1
        i1 = jnp.where(upd, k, i1)
        g1 = jnp.where(upd, cand_sc[k], g1)
        m1 = jnp.where(upd, cand_sel[k], m1)
    m2 = jnp.full_like(m1, -jnp.inf)
    i2, g2 = jnp.full_like(bg, -1), jnp.zeros_like(g1)
    for k in range(epg):
        upd = (i1 != k) & ((cand_sel[k] > m2) | (i2 < 0))
        i2 = jnp.where(upd, k, i2)
        g2 = jnp.where(upd, cand_sc[k], g2)
        m2 = jnp.where(upd, cand_sel[k], m2)
    den = g1 + g2
    route_ref[...] = jnp.zeros_like(route_ref)
    route_ref[0:1, :] = (bg * epg + i1).astype(F32)
    route_ref[1:2, :] = (bg * epg + i2).astype(F32)
    route_ref[2:3, :] = g1 / den
    route_ref[3:4, :] = g2 / den


def out_projection(x, gdn, att, ssd, mod, mod_row, g, w_out, w_router, b_router, tm=512):
    n_tok = x.shape[0]
    return pl.pallas_call(
        _outproj_kernel,
        grid=(n_tok // tm,),
        in_specs=[
            pl.BlockSpec((tm, D_MODEL), lambda i: (i, 0)),
            pl.BlockSpec((tm, GDN_WIDTH), lambda i: (i, 0)),
            pl.BlockSpec((tm, ATT_WIDTH), lambda i: (i, 0)),
            pl.BlockSpec((tm, SSD_WIDTH), lambda i: (i, 0)),
            pl.BlockSpec((None, 6, D_MODEL), lambda i: (mod_row(tm)(i), 0, 0)),
            pl.BlockSpec((1, D_MODEL), lambda i: (0, 0)),
            pl.BlockSpec((MIX_WIDTH, D_MODEL), lambda i: (0, 0)),
            pl.BlockSpec((LANES, D_MODEL), lambda i: (0, 0)),
            pl.BlockSpec((LANES, 1), lambda i: (0, 0)),
        ],
        out_specs=[
            pl.BlockSpec((tm, D_MODEL), lambda i: (i, 0)),
            pl.BlockSpec((tm, D_MODEL), lambda i: (i, 0)),
            pl.BlockSpec((SUBLANES, tm), lambda i: (0, i)),
        ],
        out_shape=[jax.ShapeDtypeStruct((n_tok, D_MODEL), F32),
                   jax.ShapeDtypeStruct((n_tok, D_MODEL), F32),
                   jax.ShapeDtypeStruct((SUBLANES, n_tok), F32)],
        compiler_params=_params("parallel"),
        name="out_projection",
    )(x, gdn, att, ssd, mod, g, w_out, w_router, b_router)


def _expert_kernel(idx_ref, be_ref, nb_ref, h_ref, wg_ref, wu_ref, wd_ref, o_ref, xbuf, sem, *, tm):
    i = pl.program_id(0)
    n_used = nb_ref[0]

    def row_copy(step, slot, r):
        return pltpu.make_async_copy(h_ref.at[pl.ds(idx_ref[step * tm + r], 1)],
                                     xbuf.at[slot, pl.ds(r, 1)], sem.at[slot])

    def start_rows(step, slot):
        def body(r, carry):
            row_copy(step, slot, r).start()
            return carry
        lax.fori_loop(0, tm, body, 0, unroll=8)

    def wait_rows(step, slot):
        def body(r, carry):
            row_copy(step, slot, r).wait()
            return carry
        lax.fori_loop(0, tm, body, 0, unroll=8)

    @pl.when(i == 0)
    def _():
        start_rows(0, 0)

    @pl.when(i + 1 < n_used)
    def _():
        start_rows(i + 1, (i + 1) % 2)

    @pl.when(i < n_used)
    def _():
        slot = i % 2
        wait_rows(i, slot)
        xb = xbuf[slot].astype(BF16)
        a = jnp.dot(xb, wg_ref[...], preferred_element_type=F32)
        u = jnp.dot(xb, wu_ref[...], preferred_element_type=F32)
        o_ref[...] = jnp.dot((_silu(a) * u).astype(BF16), wd_ref[...], preferred_element_type=F32)

    @pl.when(i >= n_used)
    def _():
        o_ref[...] = jnp.zeros_like(o_ref)


def expert_mlp(row_tok, block_e, n_used, h, w_gate, w_up, w_down, tm):
    n_rows = row_tok.shape[0]
    return pl.pallas_call(
        functools.partial(_expert_kernel, tm=tm),
        grid_spec=pltpu.PrefetchScalarGridSpec(
            num_scalar_prefetch=3,
            grid=(n_rows // tm,),
            in_specs=[
                pl.BlockSpec(memory_space=pl.ANY),
                pl.BlockSpec((None, D_MODEL, EXPERT_FF), lambda i, idx, be, nb: (be[i], 0, 0)),
                pl.BlockSpec((None, D_MODEL, EXPERT_FF), lambda i, idx, be, nb: (be[i], 0, 0)),
                pl.BlockSpec((None, EXPERT_FF, D_MODEL), lambda i, idx, be, nb: (be[i], 0, 0)),
            ],
            out_specs=pl.BlockSpec((tm, D_MODEL), lambda i, idx, be, nb: (i, 0)),
            scratch_shapes=[pltpu.VMEM((2, tm, D_MODEL), F32), pltpu.SemaphoreType.DMA((2,))],
        ),
        out_shape=jax.ShapeDtypeStruct((n_rows, D_MODEL), F32),
        compiler_params=_params("arbitrary"),
        name="expert_mlp",
    )(row_tok, block_e, n_used, h, w_gate, w_up, w_down)


def _combine_kernel(dest_ref, x_ref, gates_ref, mod_ref, y_ref, o_ref, buf, sem, *, tm, n_tok):
    base = pl.program_id(0) * tm

    def copy(k, r):
        return pltpu.make_async_copy(y_ref.at[pl.ds(dest_ref[k * n_tok + base + r], 1)],
                                     buf.at[k, pl.ds(r, 1)], sem)

    def issue(r, carry):
        copy(0, r).start()
        copy(1, r).start()
        return carry

    def drain(r, carry):
        copy(0, r).wait()
        copy(1, r).wait()
        return carry

    lax.fori_loop(0, tm, issue, 0)
    lax.fori_loop(0, tm, drain, 0)
    moe = gates_ref[:, 0:1] * buf[0] + gates_ref[:, 1:2] * buf[1]
    o_ref[...] = x_ref[...] + mod_ref[5:6, :] * moe


def moe_combine(x, y_rows, dest, gates, mod, mod_row, tm=256):
    n_tok = x.shape[0]
    return pl.pallas_call(
        functools.partial(_combine_kernel, tm=tm, n_tok=n_tok),
        grid_spec=pltpu.PrefetchScalarGridSpec(
            num_scalar_prefetch=1,
            grid=(n_tok // tm,),
            in_specs=[
                pl.BlockSpec((tm, D_MODEL), lambda i, d: (i, 0)),
                pl.BlockSpec((tm, TOP_K), lambda i, d: (i, 0)),
                pl.BlockSpec((None, 6, D_MODEL), lambda i, d: (mod_row(tm)(i), 0, 0)),
                pl.BlockSpec(memory_space=pl.ANY),
            ],
            out_specs=pl.BlockSpec((tm, D_MODEL), lambda i, d: (i, 0)),
            scratch_shapes=[pltpu.VMEM((TOP_K, tm, D_MODEL), F32), pltpu.SemaphoreType.DMA(())],
        ),
        out_shape=jax.ShapeDtypeStruct((n_tok, D_MODEL), F32),
        compiler_params=_params("arbitrary"),
        name="moe_combine",
    )(dest, x, gates, mod, y_rows)


def dispatch_plan(route, tm):
    n_tok = route.shape[1]
    e_idx = route[0:TOP_K].astype(jnp.int32)
    flat_e = e_idx.reshape(-1)
    n_assign = flat_e.shape[0]
    onehot = (flat_e[:, None] == jnp.arange(N_EXPERTS, dtype=jnp.int32)[None, :]).astype(jnp.int32)
    csum = jnp.cumsum(onehot, axis=0)
    rank = jnp.sum(onehot * csum, axis=1) - 1
    counts = csum[-1]
    padded = (counts + tm - 1) // tm * tm
    padded_end = jnp.cumsum(padded)
    padded_start = padded_end - padded
    dest = padded_start[flat_e] + rank
    n_blocks = n_assign // tm + N_EXPERTS
    n_rows = n_blocks * tm
    tok = jnp.tile(jnp.arange(n_tok, dtype=jnp.int32), TOP_K)
    row_tok = jnp.zeros((n_rows,), jnp.int32).at[dest].set(tok)
    block_e = jnp.minimum(
        jnp.searchsorted(padded_end, jnp.arange(n_blocks, dtype=jnp.int32) * tm, side='right'),
        N_EXPERTS - 1).astype(jnp.int32)
    gates = route[TOP_K:2 * TOP_K].T
    n_used = (padded_end[-1:] // tm).astype(jnp.int32)
    return row_tok, block_e, n_used, dest, gates


def _lane_vec(pairs):
    v = jnp.zeros((LANES,), F32)
    for off, arr in pairs:
        v = lax.dynamic_update_slice(v, arr.reshape(-1).astype(F32), (off,))
    return v.reshape(1, LANES)


def _rope_tables(n_tok):
    n_rows = n_tok // GRID_W
    rows = np.repeat(np.arange(n_rows), GRID_W).astype(np.float32)
    cols = np.tile(np.arange(GRID_W), n_rows).astype(np.float32)
    inv = 1.0 / (ROPE_THETA ** (jnp.arange(ROPE_AXIS_PAIRS, dtype=F32) / ROPE_AXIS_PAIRS))
    ang = jnp.concatenate([rows[:, None] * inv, cols[:, None] * inv], axis=-1)
    cos, sin = jnp.cos(ang), jnp.sin(ang)
    return jnp.concatenate([cos, cos], axis=-1), jnp.concatenate([-sin, sin], axis=-1)


def _reorder_w_in(w):
    c0 = 4 * GDN_WIDTH
    c1 = c0 + 2 * N_DIR * GDN_HEADS
    c2 = w.shape[1] - N_DIR * SSD_HEADS
    main = jnp.concatenate([w[:, :c0], w[:, c1:c2]], axis=1)
    small = jnp.concatenate([w[:, c0:c1], w[:, c2:],
                             jnp.zeros((w.shape[0], LANES - SM_USED), w.dtype)], axis=1)
    return main.astype(BF16), small.astype(BF16)


def _group_layer(x, mod, mod_row, lp, shared, n_seq, seq_len, tm_in, tq, moe_tm, ctx):
    proj, small = in_projection(x, mod, mod_row, lp['norm_mix'], lp['w_in_main'], lp['w_in_small'], tm_in)
    if ctx is None:
        att, k_new, v_new = attention(proj, lp['q_norm'], lp['k_norm'], n_seq, seq_len, tq)
        gdn, s_gdn = gated_deltanet(proj, small, lp['gdn_conv'], lp['gdn_alog_vec'], lp['gdn_dtb_vec'],
                                    lp['gdn_norm'], n_seq, seq_len)
        ssd, s_ssd = ssd_mixer(proj, small, lp['ssd_conv_w'], lp['ssd_conv_b'], lp['ssd_alog_vec'],
                               lp['ssd_dtb_vec'], lp['ssd_d_vec'], lp['ssd_norm'], n_seq, seq_len)
        new_ctx = (k_new, v_new, s_gdn, s_ssd)
    else:
        cache_k4, cache_v4, state_gdn, state_ssd, layer, cos2, sin2 = ctx
        att = attention(proj, lp['q_norm'], lp['k_norm'], n_seq, seq_len, tq,
                        ctx=(cache_k4, cache_v4, layer, cos2, sin2))[0]
        gdn, _ = gated_deltanet(proj, small, lp['gdn_conv'], lp['gdn_alog_vec'], lp['gdn_dtb_vec'],
                                lp['gdn_norm'], n_seq, seq_len, state=(state_gdn, layer))
        ssd, _ = ssd_mixer(proj, small, lp['ssd_conv_w'], lp['ssd_conv_b'], lp['ssd_alog_vec'],
                           lp['ssd_dtb_vec'], lp['ssd_d_vec'], lp['ssd_norm'], n_seq, seq_len,
                           state=(state_ssd, layer))
        new_ctx = None
    x_mid, h2, route = out_projection(x, gdn, att, ssd, mod, mod_row, lp['norm_ffn'], lp['w_out'],
                                      shared['w_router'], shared['b_router'])
    row_tok, block_e, n_used, dest, gates = dispatch_plan(route, moe_tm)
    y_rows = expert_mlp(row_tok, block_e, n_used, h2, lp['w_gate'], lp['w_up'], lp['w_down'], moe_tm)
    x_out = moe_combine(x_mid, y_rows, dest, gates, mod, mod_row)
    return x_out, new_ctx


def kernel(x_prompt, x_sample, c, cache_k, cache_v, state_gdn, state_ssd, c_ctx, w_ada, b_ada, norm_mix,
           norm_ffn, w_in, gdn_conv, gdn_a_log, gdn_dt_bias, gdn_norm, q_norm, k_norm, ssd_conv_w,
           ssd_conv_b, ssd_a_log, ssd_dt_bias, ssd_d, ssd_norm, w_out, w_router, b_router, w_gate, w_up,
           w_down):
    n_ctx, ctx_len, _ = x_prompt.shape
    n_lat, lat_len, _ = x_sample.shape
    past = cache_k.shape[2]
    kv_w = ATT_KV_HEADS * HEAD_DIM

    cond = jnp.concatenate([c_ctx[None, :], c, jnp.zeros((SUBLANES - 1 - n_lat, D_MODEL), F32)], axis=0)
    mod_all = ada_modulation(cond, w_ada, b_ada).reshape(DEPTH, SUBLANES, 6, D_MODEL)

    shared = {
        'w_router': jnp.concatenate([w_router.T, jnp.zeros((LANES - N_EXPERTS, D_MODEL), F32)],
                                    axis=0).astype(BF16),
        'b_router': jnp.concatenate([b_router, jnp.zeros((LANES - N_EXPERTS,), F32)]).reshape(LANES, 1),
    }
    cos2, sin2 = _rope_tables(lat_len)
    cache_k4 = cache_k.reshape(n_lat, DEPTH, past, kv_w)
    cache_v4 = cache_v.reshape(n_lat, DEPTH, past, kv_w)

    tm_ctx = 1024
    tm_lat = 1024
    lat_blocks_per_seq_in = lat_len // tm_lat

    y_ctx = x_prompt.reshape(n_ctx * ctx_len, D_MODEL)
    y_lat = x_sample.reshape(n_lat * lat_len, D_MODEL)
    new_k, new_v, new_gdn, new_ssd = [], [], [], []
    for l in range(DEPTH):
        w_main, w_small = _reorder_w_in(w_in[l])
        lp = {
            'norm_mix': norm_mix[l].reshape(1, D_MODEL), 'norm_ffn': norm_ffn[l].reshape(1, D_MODEL),
            'w_in_main': w_main, 'w_in_small': w_small,
            'gdn_conv': gdn_conv[l],
            'gdn_alog_vec': _lane_vec([(SM_DECAY, gdn_a_log[l])]),
            'gdn_dtb_vec': _lane_vec([(SM_DECAY, gdn_dt_bias[l])]),
            'gdn_norm': gdn_norm[l], 'q_norm': q_norm[l], 'k_norm': k_norm[l],
            'ssd_conv_w': ssd_conv_w[l], 'ssd_conv_b': ssd_conv_b[l],
            'ssd_alog_vec': _lane_vec([(SM_DT, ssd_a_log[l])]),
            'ssd_dtb_vec': _lane_vec([(SM_DT, ssd_dt_bias[l])]),
            'ssd_d_vec': jnp.repeat(ssd_d[l], SSD_HEAD_DIM).reshape(1, SSD_WIDTH),
            'ssd_norm': ssd_norm[l],
            'w_out': w_out[l].astype(BF16),
            'w_gate': w_gate[l].astype(BF16), 'w_up': w_up[l].astype(BF16), 'w_down': w_down[l].astype(BF16),
        }
        mod = mod_all[l]
        y_ctx, (k_l, v_l, g_l, s_l) = _group_layer(
            y_ctx, mod, lambda tm: (lambda i: 0), lp, shared, n_ctx, ctx_len, tm_ctx, ctx_len, 256, None)
        new_k.append(k_l.reshape(n_ctx, ctx_len, ATT_KV_HEADS, HEAD_DIM))
        new_v.append(v_l.reshape(n_ctx, ctx_len, ATT_KV_HEADS, HEAD_DIM))
        new_gdn.append(g_l)
        new_ssd.append(s_l)
        ctx = (cache_k4, cache_v4, state_gdn, state_ssd, l, cos2, sin2)
        y_lat, _ = _group_layer(
            y_lat, mod, lambda tm: (lambda i: 1 + i // (lat_len // tm)), lp, shared, n_lat, lat_len,
            tm_lat, 256, 256, ctx)
    return (y_ctx.reshape(n_ctx, ctx_len, D_MODEL), y_lat.reshape(n_lat, lat_len, D_MODEL),
            jnp.stack(new_k, axis=1), jnp.stack(new_v, axis=1),
            jnp.stack(new_gdn, axis=1), jnp.stack(new_ssd, axis=1))
```

```python
import functools

import numpy as np
import jax
import jax.numpy as jnp
from jax import lax
from jax.experimental import pallas as pl
from jax.experimental.pallas import tpu as pltpu

F32 = jnp.float32
BF16 = jnp.bfloat16

D_MODEL = 2048
DEPTH = 2
GRID_W = 64
N_DIR = 2
EPS = 1e-6
CONV_TAPS = 5
GDN_HEADS = 4
GDN_DK = 128
GDN_DV = 128
GDN_WIDTH = GDN_HEADS * GDN_DV
ATT_HEADS = 8
ATT_KV_HEADS = 2
HEAD_DIM = 128
ATT_GROUP = ATT_HEADS // ATT_KV_HEADS
ATT_WIDTH = ATT_HEADS * HEAD_DIM
ROPE_THETA = 10000.0
ROPE_AXIS_PAIRS = HEAD_DIM // 4
SSD_HEADS = 8
SSD_HEAD_DIM = 64
SSD_WIDTH = SSD_HEADS * SSD_HEAD_DIM
SSD_GROUPS = 2
SSD_HEADS_PER_GROUP = SSD_HEADS // SSD_GROUPS
SSD_GROUP_WIDTH = SSD_WIDTH // SSD_GROUPS
SSD_STATE = 128
MIX_WIDTH = GDN_WIDTH + ATT_WIDTH + SSD_WIDTH
N_EXPERTS = 16
N_EXPERT_GROUPS = 4
EXPERTS_PER_GROUP = N_EXPERTS // N_EXPERT_GROUPS
TOP_K = 2
EXPERT_FF = 1024

LANES = 128
SUBLANES = 8
CHUNK = 128
GDN_PREP_CHUNKS = 8
VMEM_LIMIT = 56 * 1024 * 1024

OFF_GQKV = 0
OFF_GGATE = 3 * GDN_WIDTH
OFF_AQ = OFF_GGATE + GDN_WIDTH
OFF_AK = OFF_AQ + ATT_WIDTH
OFF_AV = OFF_AK + ATT_KV_HEADS * HEAD_DIM
OFF_SZ = OFF_AV + ATT_KV_HEADS * HEAD_DIM
OFF_SXBC = OFF_SZ + SSD_WIDTH
MAIN_COLS = OFF_SXBC + SSD_WIDTH + 2 * SSD_GROUPS * SSD_STATE
SM_BETA = 0
SM_DECAY = SM_BETA + N_DIR * GDN_HEADS
SM_DT = SM_DECAY + N_DIR * GDN_HEADS
SM_USED = SM_DT + N_DIR * SSD_HEADS


def _silu(x):
    return x * jax.nn.sigmoid(x)


def _softplus(x):
    return jnp.maximum(x, 0.0) + jnp.log1p(jnp.exp(-jnp.abs(x)))


def _dot(a, b):
    return jnp.dot(a.astype(BF16), b.astype(BF16), preferred_element_type=F32)


def _dot_nt(a, b):
    return lax.dot_general(a.astype(BF16), b.astype(BF16), (((1,), (1,)), ((), ())),
                           preferred_element_type=F32)


def _dot_tn(a, b):
    return lax.dot_general(a.astype(BF16), b.astype(BF16), (((0,), (0,)), ((), ())),
                           preferred_element_type=F32)


def _split3(x):
    hi = x.astype(BF16)
    r1 = x - hi.astype(F32)
    mid = r1.astype(BF16)
    lo = (r1 - mid.astype(F32)).astype(BF16)
    return hi, mid, lo


def _params(*sem):
    return pltpu.CompilerParams(dimension_semantics=sem, vmem_limit_bytes=VMEM_LIMIT)


def _mod_kernel(c_ref, w_ref, b_ref, o_ref):
    s = _silu(c_ref[...])
    o_ref[...] = _dot(s, w_ref[...]) + b_ref[...]


def ada_modulation(cond, w_ada, b_ada, tn=1024):
    n_out = 6 * D_MODEL
    return pl.pallas_call(
        _mod_kernel,
        grid=(DEPTH, n_out // tn),
        in_specs=[
            pl.BlockSpec((SUBLANES, D_MODEL), lambda l, j: (0, 0)),
            pl.BlockSpec((None, D_MODEL, tn), lambda l, j: (l, 0, j)),
            pl.BlockSpec((None, 1, tn), lambda l, j: (l, 0, j)),
        ],
        out_specs=pl.BlockSpec((None, SUBLANES, tn), lambda l, j: (l, 0, j)),
        out_shape=jax.ShapeDtypeStruct((DEPTH, SUBLANES, n_out), F32),
        compiler_params=_params("parallel", "parallel"),
        name="ada_modulation",
    )(cond, w_ada, b_ada.reshape(DEPTH, 1, n_out))


def _inproj_kernel(x_ref, mod_ref, g_ref, w_ref, ws_ref, o_ref, os_ref, h_scr):
    @pl.when(pl.program_id(1) == 0)
    def _():
        x = x_ref[...]
        y = x * lax.rsqrt(jnp.mean(x * x, axis=-1, keepdims=True) + EPS)
        h = (y * g_ref[...]) * (1.0 + mod_ref[1:2, :]) + mod_ref[0:1, :]
        hb = h.astype(BF16)
        h_scr[...] = hb
        os_ref[...] = jnp.dot(hb, ws_ref[...], preferred_element_type=F32)

    o_ref[...] = jnp.dot(h_scr[...], w_ref[...], preferred_element_type=F32)


def in_projection(x, mod, mod_row, g, w_main, w_small, tm, tn=1280):
    n_tok = x.shape[0]
    return pl.pallas_call(
        _inproj_kernel,
        grid=(n_tok // tm, MAIN_COLS // tn),
        in_specs=[
            pl.BlockSpec((tm, D_MODEL), lambda i, j: (i, 0)),
            pl.BlockSpec((None, 6, D_MODEL), lambda i, j: (mod_row(tm)(i), 0, 0)),
            pl.BlockSpec((1, D_MODEL), lambda i, j: (0, 0)),
            pl.BlockSpec((D_MODEL, tn), lambda i, j: (0, j)),
            pl.BlockSpec((D_MODEL, LANES), lambda i, j: (0, 0)),
        ],
        out_specs=[
            pl.BlockSpec((tm, tn), lambda i, j: (i, j)),
            pl.BlockSpec((tm, LANES), lambda i, j: (i, 0)),
        ],
        out_shape=[jax.ShapeDtypeStruct((n_tok, MAIN_COLS), F32),
                   jax.ShapeDtypeStruct((n_tok, LANES), F32)],
        scratch_shapes=[pltpu.VMEM((tm, D_MODEL), BF16)],
        compiler_params=_params("parallel", "arbitrary"),
        name="in_projection",
    )(x, mod, g, w_main, w_small)


def _rms_head(x, g):
    return x * lax.rsqrt(jnp.mean(x * x, axis=-1, keepdims=True) + EPS) * g


def _rope(x, cos2, sin2):
    return x * cos2 + pltpu.roll(x, HEAD_DIM // 2, 1) * sin2


def _attn_kernel(*refs, has_ctx):
    if has_ctx:
        (q_ref, k_ref, v_ref, qn_ref, kn_ref, cq_ref, sq_ref, ck_ref, sk_ref, xk_ref, xv_ref,
         o_ref, k_scr, v_scr) = refs
    else:
        (q_ref, k_ref, v_ref, qn_ref, kn_ref, o_ref, ko_ref, vo_ref, k_scr, v_scr) = refs

    @pl.when(pl.program_id(2) == 0)
    def _():
        kn = _rms_head(k_ref[...], kn_ref[...])
        v = v_ref[...]
        if has_ctx:
            kn = _rope(kn, ck_ref[...], sk_ref[...])
        else:
            ko_ref[...] = kn
            vo_ref[...] = v
        k_scr[...] = kn.astype(BF16)
        v_scr[...] = v.astype(BF16)

    kb = k_scr[...]
    vb = v_scr[...]
    if has_ctx:
        xk = xk_ref[...].astype(BF16)
        xv = xv_ref[...].astype(BF16)
    scale = HEAD_DIM ** -0.5 * np.log2(np.e)

    def scores(h):
        q = _rms_head(q_ref[:, h * HEAD_DIM:(h + 1) * HEAD_DIM], qn_ref[...])
        if has_ctx:
            q = _rope(q, cq_ref[...], sq_ref[...])
        qb = (q * scale).astype(BF16)
        return _dot_nt(qb, kb), (_dot_nt(qb, xk) if has_ctx else None)

    def finish(h, s1, s2):
        m = jnp.max(s1, axis=-1, keepdims=True)
        if has_ctx:
            m = jnp.maximum(m, jnp.max(s2, axis=-1, keepdims=True))
        p1 = jnp.exp2(s1 - m)
        den = jnp.sum(p1, axis=-1, keepdims=True)
        acc = jnp.dot(p1.astype(BF16), vb, preferred_element_type=F32)
        if has_ctx:
            p2 = jnp.exp2(s2 - m)
            den = den + jnp.sum(p2, axis=-1, keepdims=True)
            acc = acc + jnp.dot(p2.astype(BF16), xv, preferred_element_type=F32)
        o_ref[:, h * HEAD_DIM:(h + 1) * HEAD_DIM] = (acc / den).astype(o_ref.dtype)

    nxt = scores(0)
    for h in range(ATT_GROUP):
        cur = nxt
        if h + 1 < ATT_GROUP:
            nxt = scores(h + 1)
        finish(h, *cur)


def attention(proj, q_norm, k_norm, n_seq, seq_len, tq, ctx=None):
    n_tok = n_seq * seq_len
    nq = seq_len // tq
    gw = ATT_GROUP * HEAD_DIM
    has_ctx = ctx is not None
    in_specs = [
        pl.BlockSpec((tq, gw), lambda b, g, i: (b * nq + i, OFF_AQ // gw + g)),
        pl.BlockSpec((seq_len, HEAD_DIM), lambda b, g, i: (b, OFF_AK // HEAD_DIM + g)),
        pl.BlockSpec((seq_len, HEAD_DIM), lambda b, g, i: (b, OFF_AV // HEAD_DIM + g)),
        pl.BlockSpec((1, HEAD_DIM), lambda b, g, i: (0, 0)),
        pl.BlockSpec((1, HEAD_DIM), lambda b, g, i: (0, 0)),
    ]
    args = [proj, proj, proj, q_norm.reshape(1, HEAD_DIM), k_norm.reshape(1, HEAD_DIM)]
    out_specs = [pl.BlockSpec((tq, gw), lambda b, g, i: (b * nq + i, g))]
    out_shape = [jax.ShapeDtypeStruct((n_tok, ATT_WIDTH), BF16)]
    if has_ctx:
        cache_k4, cache_v4, layer, cos2, sin2 = ctx
        past = cache_k4.shape[2]
        in_specs += [
            pl.BlockSpec((tq, HEAD_DIM), lambda b, g, i: (i, 0)),
            pl.BlockSpec((tq, HEAD_DIM), lambda b, g, i: (i, 0)),
            pl.BlockSpec((seq_len, HEAD_DIM), lambda b, g, i: (0, 0)),
            pl.BlockSpec((seq_len, HEAD_DIM), lambda b, g, i: (0, 0)),
            pl.BlockSpec((None, None, past, HEAD_DIM), lambda b, g, i: (b, layer, 0, g)),
            pl.BlockSpec((None, None, past, HEAD_DIM), lambda b, g, i: (b, layer, 0, g)),
        ]
        args += [cos2, sin2, cos2, sin2, cache_k4, cache_v4]
    else:
        kv_w = ATT_KV_HEADS * HEAD_DIM
        out_specs += [pl.BlockSpec((seq_len, HEAD_DIM), lambda b, g, i: (b, g)),
                      pl.BlockSpec((seq_len, HEAD_DIM), lambda b, g, i: (b, g))]
        out_shape += [jax.ShapeDtypeStruct((n_tok, kv_w), F32),
                      jax.ShapeDtypeStruct((n_tok, kv_w), F32)]
    return pl.pallas_call(
        functools.partial(_attn_kernel, has_ctx=has_ctx),
        grid=(n_seq, ATT_KV_HEADS, nq),
        in_specs=in_specs,
        out_specs=out_specs,
        out_shape=out_shape,
        scratch_shapes=[pltpu.VMEM((seq_len, HEAD_DIM), BF16), pltpu.VMEM((seq_len, HEAD_DIM), BF16)],
        compiler_params=_params("parallel", "parallel", "arbitrary"),
        name="attention_latent" if has_ctx else "attention_context",
    )(*args)


def _iota2(shape, dim):
    return lax.broadcasted_iota(jnp.int32, shape, dim)


def _conv_chunk(x_ref, w, c, n_chunks, bias=None):
    seq_len = n_chunks * CHUNK
    r0 = pl.multiple_of(c * CHUNK, CHUNK)
    cur = x_ref[pl.ds(r0, CHUNK), :]
    lo = pl.multiple_of(jnp.maximum(r0 - SUBLANES, 0), SUBLANES)
    hi = pl.multiple_of(jnp.minimum(r0 + CHUNK, seq_len - SUBLANES), SUBLANES)
    prev = jnp.where(c > 0, x_ref[pl.ds(lo, SUBLANES), :], 0.0)
    nxt = jnp.where(c < n_chunks - 1, x_ref[pl.ds(hi, SUBLANES), :], 0.0)
    ext = jnp.concatenate([prev, cur, nxt], axis=0)
    n_ext = CHUNK + 2 * SUBLANES
    pad = CONV_TAPS // 2
    acc = None
    for j in range(CONV_TAPS):
        shifted = ext if j == pad else pltpu.roll(ext, (pad - j) % n_ext, 0)
        term = shifted[SUBLANES:SUBLANES + CHUNK, :] * w[j:j + 1, :]
        acc = term if acc is None else acc + term
    if bias is not None:
        acc = acc + bias
    return acc


def _chunk_cumsums(a):
    tril = (_iota2((CHUNK, CHUNK), 1) <= _iota2((CHUNK, CHUNK), 0)).astype(BF16)
    hi, mid, lo = _split3(a)
    pre = (jnp.dot(tril, hi, preferred_element_type=F32)
           + jnp.dot(tril, mid, preferred_element_type=F32)
           + jnp.dot(tril, lo, preferred_element_type=F32))
    tot = pre[CHUNK - 1:CHUNK, :]
    suf = tot - pre + a
    return pre, suf, tot


def _pick_col(x, lane_ids, c):
    return jnp.sum(jnp.where(lane_ids == c, x, 0.0), axis=1, keepdims=True)


def _pick_row(x, row_ids, r):
    return jnp.sum(jnp.where(row_ids == r, x, 0.0), axis=0, keepdims=True)


def _scan_order_cumsum(a_col, a_row, lower):
    ii = _iota2((CHUNK, CHUNK), 0)
    jj = _iota2((CHUNK, CHUNK), 1)
    tril = (jj <= ii).astype(BF16)
    triu = (jj >= ii).astype(BF16)
    left, right = (tril, triu) if lower else (triu, tril)
    g_col = sum(jnp.dot(left, p, preferred_element_type=F32)
                for p in _split3(jnp.broadcast_to(a_col, (CHUNK, CHUNK))))
    g_row = sum(jnp.dot(p, right, preferred_element_type=F32)
                for p in _split3(jnp.broadcast_to(a_row, (CHUNK, CHUNK))))
    tot = g_col[CHUNK - 1:CHUNK, :] if lower else g_col[0:1, :]
    return g_col, g_row, tot


def _decay_matrix(g_col, g_row, incl):
    d = g_col - g_row
    return jnp.where(incl, jnp.exp(jnp.where(incl, d, 0.0)), 0.0)


def _unit_tri_inverses(lms, lowers):
    ii = _iota2((CHUNK, CHUNK), 0)
    jj = _iota2((CHUNK, CHUNK), 1)
    xs = None
    s = 1
    while s < CHUNK:
        sh = s.bit_length() - 1
        bi = ii >> sh
        bj = jj >> sh
        same = (bi >> 1) == (bj >> 1)
        m_lower = same & ((bi & 1) == 1) & ((bj & 1) == 0)
        m_upper = same & ((bi & 1) == 0) & ((bj & 1) == 1)
        offs = [jnp.where(m_lower if lower else m_upper, lm, 0.0) for lm, lower in zip(lms, lowers)]
        if xs is None:
            eye = (ii == jj).astype(F32)
            xs = [eye - off for off in offs]
        else:
            ps = [_dot(off, x) for off, x in zip(offs, xs)]
            xs = [x - _dot(x, p) for x, p in zip(xs, ps)]
        s *= 2
    return xs


def _gdn_kernel(*refs, n_chunks, has_state):
    (q_ref, k_ref, v_ref, gate_ref, sm_ref, wq_ref, wk_ref, wv_ref, alog_ref, dtb_ref, gn_ref) = refs[:11]
    pos = 11
    if has_state:
        s0_ref = refs[pos]
        pos += 1
    out_ref, sfin_ref = refs[pos:pos + 2]
    (u_scr, w_scr, qk_scr, qg_scr, kg_scr, egl_scr, o_scr) = refs[pos + 2:]
    h = pl.program_id(1)
    lane_ids = _iota2((CHUNK, LANES), 1)
    ii = _iota2((CHUNK, CHUNK), 0)
    jj = _iota2((CHUNK, CHUNK), 1)
    neg_a = -jnp.exp(alog_ref[...])
    dtb = dtb_ref[...]
    prep_chunks = int(np.gcd(n_chunks, GDN_PREP_CHUNKS))

    def prep(i, carry):
        chains = []
        for cc in range(prep_chunks):
            c = i * prep_chunks + cc
            r0 = pl.multiple_of(c * CHUNK, CHUNK)
            q = _silu(_conv_chunk(q_ref, wq_ref[...], c, n_chunks))
            k = _silu(_conv_chunk(k_ref, wk_ref[...], c, n_chunks))
            v = _silu(_conv_chunk(v_ref, wv_ref[...], c, n_chunks))
            q = q * lax.rsqrt(jnp.sum(q * q, axis=-1, keepdims=True) + EPS) * (GDN_DK ** -0.5)
            k = k * lax.rsqrt(jnp.sum(k * k, axis=-1, keepdims=True) + EPS)
            sm = sm_ref[pl.ds(r0, CHUNK), :]
            beta_all = jax.nn.sigmoid(sm)
            la_all = neg_a * _softplus(sm + dtb)
            la_t = la_all.T
            kk = _dot_nt(k, k)
            qk = _dot_nt(q, k)
            for d in range(N_DIR):
                lower = d == 0
                col = SM_DECAY + d * GDN_HEADS + h
                g_col, g_row, g_last = _scan_order_cumsum(
                    _pick_col(la_all, lane_ids, col), _pick_row(la_t, ii, col), lower)
                beta = _pick_col(beta_all, lane_ids, SM_BETA + d * GDN_HEADS + h)
                incl = (jj <= ii) if lower else (jj >= ii)
                strict = (jj < ii) if lower else (jj > ii)
                decay = _decay_matrix(g_col, g_row, incl)
                chains.append(dict(c=c, r0=r0, d=d, q=q, k=k, v=v, qk=qk, beta=beta, decay=decay,
                                   g_col=g_col, g_last=g_last,
                                   lm=jnp.where(strict, kk * beta * decay, 0.0)))
        ts = _unit_tri_inverses([ch['lm'] for ch in chains], [ch['d'] == 0 for ch in chains])
        egs = [jnp.exp(ch['g_col']) for ch in chains]
        us = [_dot(t, ch['v'] * ch['beta']) for t, ch in zip(ts, chains)]
        ws = [_dot(t, ch['k'] * (ch['beta'] * eg)) for t, ch, eg in zip(ts, chains, egs)]
        for ch, eg, u, w in zip(chains, egs, us, ws):
            d, rows = ch['d'], pl.ds(ch['r0'], CHUNK)
            u_scr[d, rows, :] = u
            w_scr[d, rows, :] = w.astype(BF16)
            qk_scr[d, rows, :] = (ch['qk'] * ch['decay']).astype(BF16)
            qg_scr[d, rows, :] = (ch['q'] * eg).astype(BF16)
            kg_scr[d, rows, :] = (ch['k'] * jnp.exp(ch['g_last'] - ch['g_col'])).astype(BF16)
            egl_scr[d, pl.ds(ch['c'], 1), :] = jnp.exp(ch['g_last'])
        return carry

    lax.fori_loop(0, n_chunks // prep_chunks, prep, 0)
    o_scr[...] = jnp.zeros_like(o_scr)

    def scan(i, carry):
        cs = (i, n_chunks - 1 - i)
        rows = [pl.ds(pl.multiple_of(c * CHUNK, CHUNK), CHUNK) for c in cs]
        sbs = [s.astype(BF16) for s in carry]
        ws = [jnp.dot(w_scr[d, rows[d], :], sbs[d], preferred_element_type=F32) for d in range(N_DIR)]
        gs = [jnp.dot(qg_scr[d, rows[d], :], sbs[d], preferred_element_type=F32) for d in range(N_DIR)]
        vbs = [(u_scr[d, rows[d], :] - ws[d]).astype(BF16) for d in range(N_DIR)]
        os = [gs[d] + jnp.dot(qk_scr[d, rows[d], :], vbs[d], preferred_element_type=F32)
              for d in range(N_DIR)]
        new = [carry[d] * egl_scr[d, pl.ds(cs[d], 1), :] + _dot_tn(kg_scr[d, rows[d], :], vbs[d])
               for d in range(N_DIR)]
        for d in range(N_DIR):
            o_scr[rows[d], :] += os[d]
        return tuple(new)

    if has_state:
        init = (s0_ref[0], s0_ref[1])
    else:
        init = (jnp.zeros((GDN_DK, GDN_DV), F32), jnp.zeros((GDN_DK, GDN_DV), F32))
    s_f, s_b = lax.fori_loop(0, n_chunks, scan, init)
    sfin_ref[0] = s_f
    sfin_ref[1] = s_b

    def finish(c, carry):
        r0 = pl.multiple_of(c * CHUNK, CHUNK)
        o = o_scr[pl.ds(r0, CHUNK), :]
        y = o * lax.rsqrt(jnp.mean(o * o, axis=-1, keepdims=True) + EPS) * gn_ref[...]
        out_ref[pl.ds(r0, CHUNK), :] = (y * _silu(gate_ref[pl.ds(r0, CHUNK), :])).astype(out_ref.dtype)
        return carry

    lax.fori_loop(0, n_chunks, finish, 0)


def gated_deltanet(proj, small, conv_w, alog_vec, dtb_vec, gnorm, n_seq, seq_len, state=None):
    n_tok = n_seq * seq_len
    n_chunks = seq_len // CHUNK
    has_state = state is not None
    nh = GDN_HEADS
    in_specs = [
        pl.BlockSpec((seq_len, LANES), lambda b, h: (b, h)),
        pl.BlockSpec((seq_len, LANES), lambda b, h: (b, nh + h)),
        pl.BlockSpec((seq_len, LANES), lambda b, h: (b, 2 * nh + h)),
        pl.BlockSpec((seq_len, LANES), lambda b, h: (b, 3 * nh + h)),
        pl.BlockSpec((seq_len, LANES), lambda b, h: (b, 0)),
        pl.BlockSpec((CONV_TAPS, LANES), lambda b, h: (0, h)),
        pl.BlockSpec((CONV_TAPS, LANES), lambda b, h: (0, nh + h)),
        pl.BlockSpec((CONV_TAPS, LANES), lambda b, h: (0, 2 * nh + h)),
        pl.BlockSpec((1, LANES), lambda b, h: (0, 0)),
        pl.BlockSpec((1, LANES), lambda b, h: (0, 0)),
        pl.BlockSpec((1, LANES), lambda b, h: (0, 0)),
    ]
    args = [proj, proj, proj, proj, small, conv_w, conv_w, conv_w, alog_vec, dtb_vec,
            gnorm.reshape(1, GDN_DV)]
    if has_state:
        state_gdn, layer = state
        in_specs.append(pl.BlockSpec((None, None, N_DIR, None, GDN_DK, GDN_DV),
                                     lambda b, h: (b, layer, 0, h, 0, 0)))
        args.append(state_gdn)
    return pl.pallas_call(
        functools.partial(_gdn_kernel, n_chunks=n_chunks, has_state=has_state),
        grid=(n_seq, GDN_HEADS),
        in_specs=in_specs,
        out_specs=[
            pl.BlockSpec((seq_len, LANES), lambda b, h: (b, h)),
            pl.BlockSpec((None, N_DIR, None, GDN_DK, GDN_DV), lambda b, h: (b, 0, h, 0, 0)),
        ],
        out_shape=[jax.ShapeDtypeStruct((n_tok, GDN_WIDTH), BF16),
                   jax.ShapeDtypeStruct((n_seq, N_DIR, GDN_HEADS, GDN_DK, GDN_DV), F32)],
        scratch_shapes=[
            pltpu.VMEM((N_DIR, seq_len, GDN_DV), F32),
            pltpu.VMEM((N_DIR, seq_len, GDN_DK), BF16),
            pltpu.VMEM((N_DIR, seq_len, CHUNK), BF16),
            pltpu.VMEM((N_DIR, seq_len, GDN_DK), BF16),
            pltpu.VMEM((N_DIR, seq_len, GDN_DK), BF16),
            pltpu.VMEM((N_DIR, max(n_chunks, SUBLANES), LANES), F32),
            pltpu.VMEM((seq_len, GDN_DV), F32),
        ],
        compiler_params=_params("parallel", "parallel"),
        name="gdn_latent" if has_state else "gdn_context",
    )(*args)


def _ssd_kernel(*refs, n_chunks, has_state):
    (x_ref, b_ref, c_ref, z_ref, sm_ref, wx_ref, wb_ref, wc_ref, bx_ref, bb_ref, bc_ref,
     alog_ref, dtb_ref, dvec_ref, gn_ref) = refs[:15]
    pos = 15
    if has_state:
        h0_ref = refs[pos]
        pos += 1
    out_ref, hfin_ref = refs[pos:pos + 2]
    xs_scr, bs_scr, cs_scr, y_scr, h_scr = refs[pos + 2:]
    g = pl.program_id(1)
    hpg = SSD_HEADS_PER_GROUP
    P = SSD_HEAD_DIM
    lane_ids = _iota2((CHUNK, LANES), 1)
    lane_row = _iota2((1, LANES), 1)
    ii = _iota2((CHUNK, CHUNK), 0)
    jj = _iota2((CHUNK, CHUNK), 1)
    neg_a = -jnp.exp(alog_ref[...])
    dtb = dtb_ref[...]

    def prep(c, carry):
        r0 = pl.multiple_of(c * CHUNK, CHUNK)
        xs_scr[pl.ds(r0, CHUNK), :] = _silu(_conv_chunk(x_ref, wx_ref[...], c, n_chunks, bx_ref[...]))
        bs_scr[pl.ds(r0, CHUNK), :] = _silu(_conv_chunk(b_ref, wb_ref[...], c, n_chunks, bb_ref[...]))
        cs_scr[pl.ds(r0, CHUNK), :] = _silu(_conv_chunk(c_ref, wc_ref[...], c, n_chunks, bc_ref[...]))
        return carry

    lax.fori_loop(0, n_chunks, prep, 0)
    y_scr[...] = jnp.zeros_like(y_scr)
    if has_state:
        h_scr[...] = h0_ref[...]
    else:
        h_scr[...] = jnp.zeros_like(h_scr)

    def scan(i, carry):
        work = []
        for d, c in ((0, i), (1, n_chunks - 1 - i)):
            rows = pl.ds(pl.multiple_of(c * CHUNK, CHUNK), CHUNK)
            lower = d == 0
            x = xs_scr[rows, :]
            bm = bs_scr[rows, :]
            cm = cs_scr[rows, :]
            dt_all = _softplus(sm_ref[rows, :] + dtb)
            pre, suf, tot = _chunk_cumsums(dt_all * neg_a)
            ac = pre if lower else suf
            ac_t = ac.T
            cb = _dot_nt(cm, bm)
            incl = (jj <= ii) if lower else (jj >= ii)
            for hh in range(hpg):
                col = SM_DT + d * SSD_HEADS + g * hpg + hh
                a_col = _pick_col(ac, lane_ids, col)
                a_last = jnp.sum(jnp.where(lane_row == col, tot, 0.0), axis=1, keepdims=True)
                work.append(dict(
                    d=d, hh=hh, rows=rows,
                    xdt=x[:, hh * P:(hh + 1) * P] * _pick_col(dt_all, lane_ids, col),
                    cbd=cb * _decay_matrix(a_col, _pick_row(ac_t, ii, col), incl),
                    b_dec=bm * jnp.exp(a_last - a_col),
                    c_dec=cm * jnp.exp(a_col),
                    e_last=jnp.exp(a_last)))
        y_in = [_dot(w['cbd'], w['xdt']) for w in work]
        sts = [_dot_tn(w['xdt'], w['b_dec']) for w in work]
        y_off = [_dot_nt(w['c_dec'], h_scr[w['d'], w['hh']]) for w in work]
        for w, yi, st, yo in zip(work, y_in, sts, y_off):
            d, hh = w['d'], w['hh']
            h_scr[d, hh] = h_scr[d, hh] * w['e_last'] + st
            y_scr[w['rows'], hh * P:(hh + 1) * P] += yi + yo
        return carry

    lax.fori_loop(0, n_chunks, scan, 0)
    hfin_ref[...] = h_scr[...]

    def finish(c, carry):
        r0 = pl.multiple_of(c * CHUNK, CHUNK)
        y = y_scr[pl.ds(r0, CHUNK), :] + dvec_ref[...] * xs_scr[pl.ds(r0, CHUNK), :]
        y = y * _silu(z_ref[pl.ds(r0, CHUNK), :])
        y = y * lax.rsqrt(jnp.mean(y * y, axis=-1, keepdims=True) + EPS) * gn_ref[...]
        out_ref[pl.ds(r0, CHUNK), :] = y.astype(out_ref.dtype)
        return carry

    lax.fori_loop(0, n_chunks, finish, 0)


def ssd_mixer(proj, small, conv_w, conv_b, alog_vec, dtb_vec, d_vec, gnorm, n_seq, seq_len, state=None):
    n_tok = n_seq * seq_len
    n_chunks = seq_len // CHUNK
    has_state = state is not None
    gw = SSD_GROUP_WIDTH
    hpg = SSD_HEADS_PER_GROUP
    xb = OFF_SXBC // gw
    bb = (OFF_SXBC + SSD_WIDTH) // LANES
    cb = bb + SSD_GROUPS
    zb = OFF_SZ // gw
    in_specs = [
        pl.BlockSpec((seq_len, gw), lambda b, g: (b, xb + g)),
        pl.BlockSpec((seq_len, LANES), lambda b, g: (b, bb + g)),
        pl.BlockSpec((seq_len, LANES), lambda b, g: (b, cb + g)),
        pl.BlockSpec((seq_len, gw), lambda b, g: (b, zb + g)),
        pl.BlockSpec((seq_len, LANES), lambda b, g: (b, 0)),
        pl.BlockSpec((CONV_TAPS, gw), lambda b, g: (0, g)),
        pl.BlockSpec((CONV_TAPS, LANES), lambda b, g: (0, SSD_WIDTH // LANES + g)),
        pl.BlockSpec((CONV_TAPS, LANES), lambda b, g: (0, SSD_WIDTH // LANES + SSD_GROUPS + g)),
        pl.BlockSpec((1, gw), lambda b, g: (0, g)),
        pl.BlockSpec((1, LANES), lambda b, g: (0, SSD_WIDTH // LANES + g)),
        pl.BlockSpec((1, LANES), lambda b, g: (0, SSD_WIDTH // LANES + SSD_GROUPS + g)),
        pl.BlockSpec((1, LANES), lambda b, g: (0, 0)),
        pl.BlockSpec((1, LANES), lambda b, g: (0, 0)),
        pl.BlockSpec((1, gw), lambda b, g: (0, g)),
        pl.BlockSpec((1, gw), lambda b, g: (0, g)),
    ]
    conv_b2 = conv_b.reshape(1, -1)
    args = [proj, proj, proj, proj, small, conv_w, conv_w, conv_w, conv_b2, conv_b2, conv_b2,
            alog_vec, dtb_vec, d_vec, gnorm.reshape(1, SSD_WIDTH)]
    if has_state:
        state_ssd, layer = state
        in_specs.append(pl.BlockSpec((None, None, N_DIR, hpg, SSD_HEAD_DIM, SSD_STATE),
                                     lambda b, g: (b, layer, 0, g, 0, 0)))
        args.append(state_ssd)
    return pl.pallas_call(
        functools.partial(_ssd_kernel, n_chunks=n_chunks, has_state=has_state),
        grid=(n_seq, SSD_GROUPS),
        in_specs=in_specs,
        out_specs=[
            pl.BlockSpec((seq_len, gw), lambda b, g: (b, g)),
            pl.BlockSpec((None, N_DIR, hpg, SSD_HEAD_DIM, SSD_STATE), lambda b, g: (b, 0, g, 0, 0)),
        ],
        out_shape=[jax.ShapeDtypeStruct((n_tok, SSD_WIDTH), BF16),
                   jax.ShapeDtypeStruct((n_seq, N_DIR, SSD_HEADS, SSD_HEAD_DIM, SSD_STATE), F32)],
        scratch_shapes=[
            pltpu.VMEM((seq_len, gw), F32),
            pltpu.VMEM((seq_len, SSD_STATE), F32),
            pltpu.VMEM((seq_len, SSD_STATE), F32),
            pltpu.VMEM((seq_len, gw), F32),
            pltpu.VMEM((N_DIR, hpg, SSD_HEAD_DIM, SSD_STATE), F32),
        ],
        compiler_params=_params("parallel", "parallel"),
        name="ssd_latent" if has_state else "ssd_context",
    )(*args)


def _top2_sum(a, b, c, d):
    hi1, lo1 = jnp.maximum(a, b), jnp.minimum(a, b)
    hi2, lo2 = jnp.maximum(c, d), jnp.minimum(c, d)
    return jnp.maximum(hi1, hi2) + jnp.maximum(jnp.minimum(hi1, hi2), jnp.maximum(lo1, lo2))


def _outproj_kernel(x_ref, gdn_ref, att_ref, ssd_ref, mod_ref, g_ref, w_ref, wr_ref, br_ref,
                    xo_ref, h_ref, route_ref):
    mixed = (jnp.dot(gdn_ref[...], w_ref[0:GDN_WIDTH, :], preferred_element_type=F32)
             + jnp.dot(att_ref[...], w_ref[GDN_WIDTH:GDN_WIDTH + ATT_WIDTH, :], preferred_element_type=F32)
             + jnp.dot(ssd_ref[...], w_ref[GDN_WIDTH + ATT_WIDTH:, :], preferred_element_type=F32))
    x = x_ref[...] + mod_ref[2:3, :] * mixed
    xo_ref[...] = x
    y = x * lax.rsqrt(jnp.mean(x * x, axis=-1, keepdims=True) + EPS)
    h = (y * g_ref[...]) * (1.0 + mod_ref[4:5, :]) + mod_ref[3:4, :]
    h_ref[...] = h
    lt = _dot_nt(wr_ref[...], h)
    sc = [jax.nn.sigmoid(lt[e:e + 1, :]) for e in range(N_EXPERTS)]
    sel = [sc[e] + br_ref[e:e + 1, :] for e in range(N_EXPERTS)]
    epg = EXPERTS_PER_GROUP
    gs = [_top2_sum(*sel[gi * epg:(gi + 1) * epg]) for gi in range(N_EXPERT_GROUPS)]
    best, bg = gs[0], jnp.zeros_like(gs[0], dtype=jnp.int32)
    for gi in range(1, N_EXPERT_GROUPS):
        upd = gs[gi] > best
        bg = jnp.where(upd, gi, bg)
        best = jnp.where(upd, gs[gi], best)
    cand_sel, cand_sc = [], []
    for k in range(epg):
        vs, vc = sel[k], sc[k]
        for gi in range(1, N_EXPERT_GROUPS):
            vs = jnp.where(bg == gi, sel[gi * epg + k], vs)
            vc = jnp.where(bg == gi, sc[gi * epg + k], vc)
        cand_sel.append(vs)
        cand_sc.append(vc)
    m1, i1, g1 = cand_sel[0], jnp.zeros_like(bg), cand_sc[0]
    for k in range(1, epg):
        upd = cand_sel[k] > m1
        i1 = jnp.where(upd, k, i1)
        g1 = jnp.where(upd, cand_sc[k], g1)
        m1 = jnp.where(upd, cand_sel[k], m1)
    m2 = jnp.full_like(m1, -jnp.inf)
    i2, g2 = jnp.full_like(bg, -1), jnp.zeros_like(g1)
    for k in range(epg):
        upd = (i1 != k) & ((cand_sel[k] > m2) | (i2 < 0))
        i2 = jnp.where(upd, k, i2)
        g2 = jnp.where(upd, cand_sc[k], g2)
        m2 = jnp.where(upd, cand_sel[k], m2)
    den = g1 + g2
    route_ref[...] = jnp.zeros_like(route_ref)
    route_ref[0:1, :] = (bg * epg + i1).astype(F32)
    route_ref[1:2, :] = (bg * epg + i2).astype(F32)
    route_ref[2:3, :] = g1 / den
    route_ref[3:4, :] = g2 / den


def out_projection(x, gdn, att, ssd, mod, mod_row, g, w_out, w_router, b_router, tm=512):
    n_tok = x.shape[0]
    return pl.pallas_call(
        _outproj_kernel,
        grid=(n_tok // tm,),
        in_specs=[
            pl.BlockSpec((tm, D_MODEL), lambda i: (i, 0)),
            pl.BlockSpec((tm, GDN_WIDTH), lambda i: (i, 0)),
            pl.BlockSpec((tm, ATT_WIDTH), lambda i: (i, 0)),
            pl.BlockSpec((tm, SSD_WIDTH), lambda i: (i, 0)),
            pl.BlockSpec((None, 6, D_MODEL), lambda i: (mod_row(tm)(i), 0, 0)),
            pl.BlockSpec((1, D_MODEL), lambda i: (0, 0)),
            pl.BlockSpec((MIX_WIDTH, D_MODEL), lambda i: (0, 0)),
            pl.BlockSpec((LANES, D_MODEL), lambda i: (0, 0)),
            pl.BlockSpec((LANES, 1), lambda i: (0, 0)),
        ],
        out_specs=[
            pl.BlockSpec((tm, D_MODEL), lambda i: (i, 0)),
            pl.BlockSpec((tm, D_MODEL), lambda i: (i, 0)),
            pl.BlockSpec((SUBLANES, tm), lambda i: (0, i)),
        ],
        out_shape=[jax.ShapeDtypeStruct((n_tok, D_MODEL), F32),
                   jax.ShapeDtypeStruct((n_tok, D_MODEL), F32),
                   jax.ShapeDtypeStruct((SUBLANES, n_tok), F32)],
        compiler_params=_params("parallel"),
        name="out_projection",
    )(x, gdn, att, ssd, mod, g, w_out, w_router, b_router)


def _expert_kernel(idx_ref, be_ref, nb_ref, h_ref, wg_ref, wu_ref, wd_ref, o_ref, xbuf, sem, *, tm):
    i = pl.program_id(0)
    n_used = nb_ref[0]

    def row_copy(step, slot, r):
        return pltpu.make_async_copy(h_ref.at[pl.ds(idx_ref[step * tm + r], 1)],
                                     xbuf.at[slot, pl.ds(r, 1)], sem.at[slot])

    def start_rows(step, slot):
        def body(r2, carry):
            row_copy(step, slot, 2 * r2).start(priority=0)
            row_copy(step, slot, 2 * r2 + 1).start(priority=1)
            return carry
        lax.fori_loop(0, tm // 2, body, 0, unroll=4)

    def wait_rows(step, slot):
        def body(r, carry):
            row_copy(step, slot, r).wait()
            return carry
        lax.fori_loop(0, tm, body, 0, unroll=8)

    @pl.when(i == 0)
    def _():
        start_rows(0, 0)

    @pl.when(i + 1 < n_used)
    def _():
        start_rows(i + 1, (i + 1) % 2)

    @pl.when(i < n_used)
    def _():
        slot = i % 2
        wait_rows(i, slot)
        xb = xbuf[slot].astype(BF16)
        a = jnp.dot(xb, wg_ref[...], preferred_element_type=F32)
        u = jnp.dot(xb, wu_ref[...], preferred_element_type=F32)
        o_ref[...] = jnp.dot((_silu(a) * u).astype(BF16), wd_ref[...], preferred_element_type=F32)

    @pl.when(i >= n_used)
    def _():
        o_ref[...] = jnp.zeros_like(o_ref)


def _cast_kernel(x_ref, o_ref):
    o_ref[...] = x_ref[...].astype(o_ref.dtype)


def cast_bf16(x):
    n, r, c = x.shape
    return pl.pallas_call(
        _cast_kernel,
        grid=(n,),
        in_specs=[pl.BlockSpec((None, r, c), lambda i: (i, 0, 0))],
        out_specs=pl.BlockSpec((None, r, c), lambda i: (i, 0, 0)),
        out_shape=jax.ShapeDtypeStruct((n, r, c), BF16),
        compiler_params=_params("parallel"),
        name="cast_bf16",
    )(x)


def expert_mlp(row_tok, block_e, n_used, h, w_gate, w_up, w_down, tm, layer):
    n_rows = row_tok.shape[0]
    e0 = layer * N_EXPERTS
    return pl.pallas_call(
        functools.partial(_expert_kernel, tm=tm),
        grid_spec=pltpu.PrefetchScalarGridSpec(
            num_scalar_prefetch=3,
            grid=(n_rows // tm,),
            in_specs=[
                pl.BlockSpec(memory_space=pl.ANY),
                pl.BlockSpec((None, D_MODEL, EXPERT_FF), lambda i, idx, be, nb: (e0 + be[i], 0, 0)),
                pl.BlockSpec((None, D_MODEL, EXPERT_FF), lambda i, idx, be, nb: (e0 + be[i], 0, 0)),
                pl.BlockSpec((None, EXPERT_FF, D_MODEL), lambda i, idx, be, nb: (e0 + be[i], 0, 0)),
            ],
            out_specs=pl.BlockSpec((tm, D_MODEL), lambda i, idx, be, nb: (i, 0)),
            scratch_shapes=[pltpu.VMEM((2, tm, D_MODEL), F32), pltpu.SemaphoreType.DMA((2,))],
        ),
        out_shape=jax.ShapeDtypeStruct((n_rows, D_MODEL), F32),
        compiler_params=_params("arbitrary"),
        name="expert_mlp",
    )(row_tok, block_e, n_used, h, w_gate, w_up, w_down)


def _combine_kernel(dest_ref, x_ref, gates_ref, mod_ref, y_ref, o_ref, buf, sem, *, tm, n_tok):
    i = pl.program_id(0)

    def copy(step, slot, k, r):
        return pltpu.make_async_copy(y_ref.at[pl.ds(dest_ref[k * n_tok + step * tm + r], 1)],
                                     buf.at[slot, k, pl.ds(r, 1)], sem.at[slot])

    def start_rows(step, slot):
        def body(r, carry):
            for k in range(TOP_K):
                copy(step, slot, k, r).start(priority=k)
            return carry
        lax.fori_loop(0, tm, body, 0, unroll=4)

    def wait_rows(step, slot):
        def body(r, carry):
            for k in range(TOP_K):
                copy(step, slot, k, r).wait()
            return carry
        lax.fori_loop(0, tm, body, 0, unroll=4)

    @pl.when(i == 0)
    def _():
        start_rows(0, 0)

    @pl.when(i + 1 < pl.num_programs(0))
    def _():
        start_rows(i + 1, (i + 1) % 2)

    slot = i % 2
    wait_rows(i, slot)
    moe = gates_ref[:, 0:1] * buf[slot, 0] + gates_ref[:, 1:2] * buf[slot, 1]
    o_ref[...] = x_ref[...] + mod_ref[5:6, :] * moe


def moe_combine(x, y_rows, dest, gates, mod, mod_row, tm=256):
    n_tok = x.shape[0]
    return pl.pallas_call(
        functools.partial(_combine_kernel, tm=tm, n_tok=n_tok),
        grid_spec=pltpu.PrefetchScalarGridSpec(
            num_scalar_prefetch=1,
            grid=(n_tok // tm,),
            in_specs=[
                pl.BlockSpec((tm, D_MODEL), lambda i, d: (i, 0)),
                pl.BlockSpec((tm, TOP_K), lambda i, d: (i, 0)),
                pl.BlockSpec((None, 6, D_MODEL), lambda i, d: (mod_row(tm)(i), 0, 0)),
                pl.BlockSpec(memory_space=pl.ANY),
            ],
            out_specs=pl.BlockSpec((tm, D_MODEL), lambda i, d: (i, 0)),
            scratch_shapes=[pltpu.VMEM((2, TOP_K, tm, D_MODEL), F32), pltpu.SemaphoreType.DMA((2,))],
        ),
        out_shape=jax.ShapeDtypeStruct((n_tok, D_MODEL), F32),
        compiler_params=_params("arbitrary"),
        name="moe_combine",
    )(dest, x, gates, mod, y_rows)


def dispatch_plan(route, tm):
    n_tok = route.shape[1]
    e_idx = route[0:TOP_K].astype(jnp.int32)
    flat_e = e_idx.reshape(-1)
    n_assign = flat_e.shape[0]
    onehot = (flat_e[:, None] == jnp.arange(N_EXPERTS, dtype=jnp.int32)[None, :]).astype(jnp.int32)
    csum = jnp.cumsum(onehot, axis=0)
    rank = jnp.sum(onehot * csum, axis=1) - 1
    counts = csum[-1]
    padded = (counts + tm - 1) // tm * tm
    padded_end = jnp.cumsum(padded)
    padded_start = padded_end - padded
    dest = padded_start[flat_e] + rank
    n_blocks = n_assign // tm + N_EXPERTS
    n_rows = n_blocks * tm
    tok = jnp.tile(jnp.arange(n_tok, dtype=jnp.int32), TOP_K)
    row_tok = jnp.zeros((n_rows,), jnp.int32).at[dest].set(tok)
    block_e = jnp.minimum(
        jnp.searchsorted(padded_end, jnp.arange(n_blocks, dtype=jnp.int32) * tm, side='right'),
        N_EXPERTS - 1).astype(jnp.int32)
    gates = route[TOP_K:2 * TOP_K].T
    n_used = (padded_end[-1:] // tm).astype(jnp.int32)
    return row_tok, block_e, n_used, dest, gates


def _lane_vec(pairs):
    v = jnp.zeros((LANES,), F32)
    for off, arr in pairs:
        v = lax.dynamic_update_slice(v, arr.reshape(-1).astype(F32), (off,))
    return v.reshape(1, LANES)


def _rope_tables(n_tok):
    n_rows = n_tok // GRID_W
    rows = np.repeat(np.arange(n_rows), GRID_W).astype(np.float32)
    cols = np.tile(np.arange(GRID_W), n_rows).astype(np.float32)
    inv = 1.0 / (ROPE_THETA ** (jnp.arange(ROPE_AXIS_PAIRS, dtype=F32) / ROPE_AXIS_PAIRS))
    ang = jnp.concatenate([rows[:, None] * inv, cols[:, None] * inv], axis=-1)
    cos, sin = jnp.cos(ang), jnp.sin(ang)
    return jnp.concatenate([cos, cos], axis=-1), jnp.concatenate([-sin, sin], axis=-1)


def _reorder_w_in(w):
    c0 = 4 * GDN_WIDTH
    c1 = c0 + 2 * N_DIR * GDN_HEADS
    c2 = w.shape[1] - N_DIR * SSD_HEADS
    main = jnp.concatenate([w[:, :c0], w[:, c1:c2]], axis=1)
    small = jnp.concatenate([w[:, c0:c1], w[:, c2:],
                             jnp.zeros((w.shape[0], LANES - SM_USED), w.dtype)], axis=1)
    return main.astype(BF16), small.astype(BF16)


def _group_layer(x, mod, mod_row, lp, shared, n_seq, seq_len, tm_in, tq, moe_tm, ctx):
    proj, small = in_projection(x, mod, mod_row, lp['norm_mix'], lp['w_in_main'], lp['w_in_small'], tm_in)
    if ctx is None:
        att, k_new, v_new = attention(proj, lp['q_norm'], lp['k_norm'], n_seq, seq_len, tq)
        gdn, s_gdn = gated_deltanet(proj, small, lp['gdn_conv'], lp['gdn_alog_vec'], lp['gdn_dtb_vec'],
                                    lp['gdn_norm'], n_seq, seq_len)
        ssd, s_ssd = ssd_mixer(proj, small, lp['ssd_conv_w'], lp['ssd_conv_b'], lp['ssd_alog_vec'],
                               lp['ssd_dtb_vec'], lp['ssd_d_vec'], lp['ssd_norm'], n_seq, seq_len)
        new_ctx = (k_new, v_new, s_gdn, s_ssd)
    else:
        cache_k4, cache_v4, state_gdn, state_ssd, layer, cos2, sin2 = ctx
        att = attention(proj, lp['q_norm'], lp['k_norm'], n_seq, seq_len, tq,
                        ctx=(cache_k4, cache_v4, layer, cos2, sin2))[0]
        gdn, _ = gated_deltanet(proj, small, lp['gdn_conv'], lp['gdn_alog_vec'], lp['gdn_dtb_vec'],
                                lp['gdn_norm'], n_seq, seq_len, state=(state_gdn, layer))
        ssd, _ = ssd_mixer(proj, small, lp['ssd_conv_w'], lp['ssd_conv_b'], lp['ssd_alog_vec'],
                           lp['ssd_dtb_vec'], lp['ssd_d_vec'], lp['ssd_norm'], n_seq, seq_len,
                           state=(state_ssd, layer))
        new_ctx = None
    x_mid, h2, route = out_projection(x, gdn, att, ssd, mod, mod_row, lp['norm_ffn'], lp['w_out'],
                                      shared['w_router'], shared['b_router'])
    row_tok, block_e, n_used, dest, gates = dispatch_plan(route, moe_tm)
    y_rows = expert_mlp(row_tok, block_e, n_used, h2, shared['w_gate'], shared['w_up'], shared['w_down'],
                        moe_tm, lp['layer'])
    x_out = moe_combine(x_mid, y_rows, dest, gates, mod, mod_row)
    return x_out, new_ctx


def kernel(x_prompt, x_sample, c, cache_k, cache_v, state_gdn, state_ssd, c_ctx, w_ada, b_ada, norm_mix,
           norm_ffn, w_in, gdn_conv, gdn_a_log, gdn_dt_bias, gdn_norm, q_norm, k_norm, ssd_conv_w,
           ssd_conv_b, ssd_a_log, ssd_dt_bias, ssd_d, ssd_norm, w_out, w_router, b_router, w_gate, w_up,
           w_down):
    n_ctx, ctx_len, _ = x_prompt.shape
    n_lat, lat_len, _ = x_sample.shape
    past = cache_k.shape[2]
    kv_w = ATT_KV_HEADS * HEAD_DIM

    cond = jnp.concatenate([c_ctx[None, :], c, jnp.zeros((SUBLANES - 1 - n_lat, D_MODEL), F32)], axis=0)
    mod_all = ada_modulation(cond, w_ada, b_ada).reshape(DEPTH, SUBLANES, 6, D_MODEL)

    shared = {
        'w_router': jnp.concatenate([w_router.T, jnp.zeros((LANES - N_EXPERTS, D_MODEL), F32)],
                                    axis=0).astype(BF16),
        'b_router': jnp.concatenate([b_router, jnp.zeros((LANES - N_EXPERTS,), F32)]).reshape(LANES, 1),
    }
    cos2, sin2 = _rope_tables(lat_len)
    cache_k4 = cache_k.reshape(n_lat, DEPTH, past, kv_w)
    cache_v4 = cache_v.reshape(n_lat, DEPTH, past, kv_w)

    shared['w_gate'] = cast_bf16(w_gate.reshape(DEPTH * N_EXPERTS, D_MODEL, EXPERT_FF))
    shared['w_up'] = cast_bf16(w_up.reshape(DEPTH * N_EXPERTS, D_MODEL, EXPERT_FF))
    shared['w_down'] = cast_bf16(w_down.reshape(DEPTH * N_EXPERTS, EXPERT_FF, D_MODEL))
    w_out_b = cast_bf16(w_out)
    tm_ctx = 1024
    tm_lat = 1024

    y_ctx = x_prompt.reshape(n_ctx * ctx_len, D_MODEL)
    y_lat = x_sample.reshape(n_lat * lat_len, D_MODEL)
    new_k, new_v, new_gdn, new_ssd = [], [], [], []
    for l in range(DEPTH):
        w_main, w_small = _reorder_w_in(w_in[l])
        lp = {
            'norm_mix': norm_mix[l].reshape(1, D_MODEL), 'norm_ffn': norm_ffn[l].reshape(1, D_MODEL),
            'w_in_main': w_main, 'w_in_small': w_small,
            'gdn_conv': gdn_conv[l],
            'gdn_alog_vec': _lane_vec([(SM_DECAY, gdn_a_log[l])]),
            'gdn_dtb_vec': _lane_vec([(SM_DECAY, gdn_dt_bias[l])]),
            'gdn_norm': gdn_norm[l], 'q_norm': q_norm[l], 'k_norm': k_norm[l],
            'ssd_conv_w': ssd_conv_w[l], 'ssd_conv_b': ssd_conv_b[l],
            'ssd_alog_vec': _lane_vec([(SM_DT, ssd_a_log[l])]),
            'ssd_dtb_vec': _lane_vec([(SM_DT, ssd_dt_bias[l])]),
            'ssd_d_vec': jnp.repeat(ssd_d[l], SSD_HEAD_DIM).reshape(1, SSD_WIDTH),
            'ssd_norm': ssd_norm[l],
            'w_out': w_out_b[l], 'layer': l,
        }
        mod = mod_all[l]
        y_ctx, (k_l, v_l, g_l, s_l) = _group_layer(
            y_ctx, mod, lambda tm: (lambda i: 0), lp, shared, n_ctx, ctx_len, tm_ctx, ctx_len, 256, None)
        new_k.append(k_l.reshape(n_ctx, ctx_len, ATT_KV_HEADS, HEAD_DIM))
        new_v.append(v_l.reshape(n_ctx, ctx_len, ATT_KV_HEADS, HEAD_DIM))
        new_gdn.append(g_l)
        new_ssd.append(s_l)
        ctx = (cache_k4, cache_v4, state_gdn, state_ssd, l, cos2, sin2)
        y_lat, _ = _group_layer(
            y_lat, mod, lambda tm: (lambda i: 1 + i // (lat_len // tm)), lp, shared, n_lat, lat_len,
            tm_lat, 256, 256, ctx)
    return (y_ctx.reshape(n_ctx, ctx_len, D_MODEL), y_lat.reshape(n_lat, lat_len, D_MODEL),
            jnp.stack(new_k, axis=1), jnp.stack(new_v, axis=1),
            jnp.stack(new_gdn, axis=1), jnp.stack(new_ssd, axis=1))
```

```python
import functools

import numpy as np
import jax
import jax.numpy as jnp
from jax import lax
from jax.experimental import pallas as pl
from jax.experimental.pallas import tpu as pltpu

F32 = jnp.float32
BF16 = jnp.bfloat16

D_MODEL = 2048
DEPTH = 2
GRID_W = 64
N_DIR = 2
EPS = 1e-6
CONV_TAPS = 5
GDN_HEADS = 4
GDN_DK = 128
GDN_DV = 128
GDN_WIDTH = GDN_HEADS * GDN_DV
ATT_HEADS = 8
ATT_KV_HEADS = 2
HEAD_DIM = 128
ATT_GROUP = ATT_HEADS // ATT_KV_HEADS
ATT_WIDTH = ATT_HEADS * HEAD_DIM
ROPE_THETA = 10000.0
ROPE_AXIS_PAIRS = HEAD_DIM // 4
SSD_HEADS = 8
SSD_HEAD_DIM = 64
SSD_WIDTH = SSD_HEADS * SSD_HEAD_DIM
SSD_GROUPS = 2
SSD_HEADS_PER_GROUP = SSD_HEADS // SSD_GROUPS
SSD_GROUP_WIDTH = SSD_WIDTH // SSD_GROUPS
SSD_STATE = 128
MIX_WIDTH = GDN_WIDTH + ATT_WIDTH + SSD_WIDTH
N_EXPERTS = 16
N_EXPERT_GROUPS = 4
EXPERTS_PER_GROUP = N_EXPERTS // N_EXPERT_GROUPS
TOP_K = 2
EXPERT_FF = 1024

LANES = 128
SUBLANES = 8
CHUNK = 128
GDN_PREP_CHUNKS = 8
VMEM_LIMIT = 56 * 1024 * 1024

OFF_GQKV = 0
OFF_GGATE = 3 * GDN_WIDTH
OFF_AQ = OFF_GGATE + GDN_WIDTH
OFF_AK = OFF_AQ + ATT_WIDTH
OFF_AV = OFF_AK + ATT_KV_HEADS * HEAD_DIM
OFF_SZ = OFF_AV + ATT_KV_HEADS * HEAD_DIM
OFF_SXBC = OFF_SZ + SSD_WIDTH
MAIN_COLS = OFF_SXBC + SSD_WIDTH + 2 * SSD_GROUPS * SSD_STATE
SM_BETA = 0
SM_DECAY = SM_BETA + N_DIR * GDN_HEADS
SM_DT = SM_DECAY + N_DIR * GDN_HEADS
SM_USED = SM_DT + N_DIR * SSD_HEADS


def _silu(x):
    return x * jax.nn.sigmoid(x)


def _softplus(x):
    return jnp.maximum(x, 0.0) + jnp.log1p(jnp.exp(-jnp.abs(x)))


def _dot(a, b):
    return jnp.dot(a.astype(BF16), b.astype(BF16), preferred_element_type=F32)


def _dot_nt(a, b):
    return lax.dot_general(a.astype(BF16), b.astype(BF16), (((1,), (1,)), ((), ())),
                           preferred_element_type=F32)


def _dot_tn(a, b):
    return lax.dot_general(a.astype(BF16), b.astype(BF16), (((0,), (0,)), ((), ())),
                           preferred_element_type=F32)


def _split3(x):
    hi = x.astype(BF16)
    r1 = x - hi.astype(F32)
    mid = r1.astype(BF16)
    lo = (r1 - mid.astype(F32)).astype(BF16)
    return hi, mid, lo


def _params(*sem):
    return pltpu.CompilerParams(dimension_semantics=sem, vmem_limit_bytes=VMEM_LIMIT)


def _mod_kernel(c_ref, w_ref, b_ref, o_ref):
    s = _silu(c_ref[...])
    o_ref[...] = _dot(s, w_ref[...]) + b_ref[...]


def ada_modulation(cond, w_ada, b_ada, tn=1024):
    n_out = 6 * D_MODEL
    return pl.pallas_call(
        _mod_kernel,
        grid=(DEPTH, n_out // tn),
        in_specs=[
            pl.BlockSpec((SUBLANES, D_MODEL), lambda l, j: (0, 0)),
            pl.BlockSpec((None, D_MODEL, tn), lambda l, j: (l, 0, j)),
            pl.BlockSpec((None, 1, tn), lambda l, j: (l, 0, j)),
        ],
        out_specs=pl.BlockSpec((None, SUBLANES, tn), lambda l, j: (l, 0, j)),
        out_shape=jax.ShapeDtypeStruct((DEPTH, SUBLANES, n_out), F32),
        compiler_params=_params("parallel", "parallel"),
        name="ada_modulation",
    )(cond, w_ada, b_ada.reshape(DEPTH, 1, n_out))


def _inproj_kernel(x_ref, mod_ref, g_ref, w_ref, ws_ref, o_ref, os_ref, h_scr):
    @pl.when(pl.program_id(1) == 0)
    def _():
        x = x_ref[...]
        y = x * lax.rsqrt(jnp.mean(x * x, axis=-1, keepdims=True) + EPS)
        h = (y * g_ref[...]) * (1.0 + mod_ref[1:2, :]) + mod_ref[0:1, :]
        hb = h.astype(BF16)
        h_scr[...] = hb
        os_ref[...] = jnp.dot(hb, ws_ref[...], preferred_element_type=F32)

    o_ref[...] = jnp.dot(h_scr[...], w_ref[...], preferred_element_type=F32)


def in_projection(x, mod, mod_row, g, w_main, w_small, tm, tn=1280):
    n_tok = x.shape[0]
    return pl.pallas_call(
        _inproj_kernel,
        grid=(n_tok // tm, MAIN_COLS // tn),
        in_specs=[
            pl.BlockSpec((tm, D_MODEL), lambda i, j: (i, 0)),
            pl.BlockSpec((None, 6, D_MODEL), lambda i, j: (mod_row(tm)(i), 0, 0)),
            pl.BlockSpec((1, D_MODEL), lambda i, j: (0, 0)),
            pl.BlockSpec((D_MODEL, tn), lambda i, j: (0, j)),
            pl.BlockSpec((D_MODEL, LANES), lambda i, j: (0, 0)),
        ],
        out_specs=[
            pl.BlockSpec((tm, tn), lambda i, j: (i, j)),
            pl.BlockSpec((tm, LANES), lambda i, j: (i, 0)),
        ],
        out_shape=[jax.ShapeDtypeStruct((n_tok, MAIN_COLS), F32),
                   jax.ShapeDtypeStruct((n_tok, LANES), F32)],
        scratch_shapes=[pltpu.VMEM((tm, D_MODEL), BF16)],
        compiler_params=_params("parallel", "arbitrary"),
        name="in_projection",
    )(x, mod, g, w_main, w_small)


def _rms_head(x, g):
    return x * lax.rsqrt(jnp.mean(x * x, axis=-1, keepdims=True) + EPS) * g


def _rope(x, cos2, sin2):
    return x * cos2 + pltpu.roll(x, HEAD_DIM // 2, 1) * sin2


def _attn_kernel(*refs, has_ctx):
    if has_ctx:
        (q_ref, k_ref, v_ref, qn_ref, kn_ref, cq_ref, sq_ref, ck_ref, sk_ref, xk_ref, xv_ref,
         o_ref, k_scr, v_scr) = refs
    else:
        (q_ref, k_ref, v_ref, qn_ref, kn_ref, o_ref, ko_ref, vo_ref, k_scr, v_scr) = refs

    @pl.when(pl.program_id(2) == 0)
    def _():
        kn = _rms_head(k_ref[...], kn_ref[...])
        v = v_ref[...]
        if has_ctx:
            kn = _rope(kn, ck_ref[...], sk_ref[...])
        else:
            ko_ref[...] = kn
            vo_ref[...] = v
        k_scr[...] = kn.astype(BF16)
        v_scr[...] = v.astype(BF16)

    kb = k_scr[...]
    vb = v_scr[...]
    if has_ctx:
        xk = xk_ref[...].astype(BF16)
        xv = xv_ref[...].astype(BF16)
    scale = HEAD_DIM ** -0.5 * np.log2(np.e)

    def scores(h):
        q = _rms_head(q_ref[:, h * HEAD_DIM:(h + 1) * HEAD_DIM], qn_ref[...])
        if has_ctx:
            q = _rope(q, cq_ref[...], sq_ref[...])
        qb = (q * scale).astype(BF16)
        return _dot_nt(qb, kb), (_dot_nt(qb, xk) if has_ctx else None)

    def softmax(s1, s2):
        m = jnp.max(s1, axis=-1, keepdims=True)
        if has_ctx:
            m = jnp.maximum(m, jnp.max(s2, axis=-1, keepdims=True))
        p1 = jnp.exp2(s1 - m)
        den = jnp.sum(p1, axis=-1, keepdims=True)
        p2 = None
        if has_ctx:
            p2 = jnp.exp2(s2 - m)
            den = den + jnp.sum(p2, axis=-1, keepdims=True)
            p2 = p2.astype(BF16)
        return p1.astype(BF16), p2, den

    def weighted_values(h, p1, p2, den):
        acc = jnp.dot(p1, vb, preferred_element_type=F32)
        if has_ctx:
            acc = acc + jnp.dot(p2, xv, preferred_element_type=F32)
        o_ref[:, h * HEAD_DIM:(h + 1) * HEAD_DIM] = (acc / den).astype(o_ref.dtype)

    s = {0: scores(0)}
    if ATT_GROUP > 1:
        s[1] = scores(1)
    p = {0: softmax(*s[0])}
    for h in range(ATT_GROUP):
        if h + 1 < ATT_GROUP:
            p[h + 1] = softmax(*s[h + 1])
        weighted_values(h, *p[h])
        if h + 2 < ATT_GROUP:
            s[h + 2] = scores(h + 2)


def attention(proj, q_norm, k_norm, n_seq, seq_len, tq, ctx=None):
    n_tok = n_seq * seq_len
    nq = seq_len // tq
    gw = ATT_GROUP * HEAD_DIM
    has_ctx = ctx is not None
    in_specs = [
        pl.BlockSpec((tq, gw), lambda b, g, i: (b * nq + i, OFF_AQ // gw + g)),
        pl.BlockSpec((seq_len, HEAD_DIM), lambda b, g, i: (b, OFF_AK // HEAD_DIM + g)),
        pl.BlockSpec((seq_len, HEAD_DIM), lambda b, g, i: (b, OFF_AV // HEAD_DIM + g)),
        pl.BlockSpec((1, HEAD_DIM), lambda b, g, i: (0, 0)),
        pl.BlockSpec((1, HEAD_DIM), lambda b, g, i: (0, 0)),
    ]
    args = [proj, proj, proj, q_norm.reshape(1, HEAD_DIM), k_norm.reshape(1, HEAD_DIM)]
    out_specs = [pl.BlockSpec((tq, gw), lambda b, g, i: (b * nq + i, g))]
    out_shape = [jax.ShapeDtypeStruct((n_tok, ATT_WIDTH), BF16)]
    if has_ctx:
        cache_k4, cache_v4, layer, cos2, sin2 = ctx
        past = cache_k4.shape[2]
        in_specs += [
            pl.BlockSpec((tq, HEAD_DIM), lambda b, g, i: (i, 0)),
            pl.BlockSpec((tq, HEAD_DIM), lambda b, g, i: (i, 0)),
            pl.BlockSpec((seq_len, HEAD_DIM), lambda b, g, i: (0, 0)),
            pl.BlockSpec((seq_len, HEAD_DIM), lambda b, g, i: (0, 0)),
            pl.BlockSpec((None, None, past, HEAD_DIM), lambda b, g, i: (b, layer, 0, g)),
            pl.BlockSpec((None, None, past, HEAD_DIM), lambda b, g, i: (b, layer, 0, g)),
        ]
        args += [cos2, sin2, cos2, sin2, cache_k4, cache_v4]
    else:
        kv_w = ATT_KV_HEADS * HEAD_DIM
        out_specs += [pl.BlockSpec((seq_len, HEAD_DIM), lambda b, g, i: (b, g)),
                      pl.BlockSpec((seq_len, HEAD_DIM), lambda b, g, i: (b, g))]
        out_shape += [jax.ShapeDtypeStruct((n_tok, kv_w), F32),
                      jax.ShapeDtypeStruct((n_tok, kv_w), F32)]
    return pl.pallas_call(
        functools.partial(_attn_kernel, has_ctx=has_ctx),
        grid=(n_seq, ATT_KV_HEADS, nq),
        in_specs=in_specs,
        out_specs=out_specs,
        out_shape=out_shape,
        scratch_shapes=[pltpu.VMEM((seq_len, HEAD_DIM), BF16), pltpu.VMEM((seq_len, HEAD_DIM), BF16)],
        compiler_params=_params("parallel", "parallel", "arbitrary"),
        name="attention_latent" if has_ctx else "attention_context",
    )(*args)


def _iota2(shape, dim):
    return lax.broadcasted_iota(jnp.int32, shape, dim)


def _conv_chunk(x_ref, w, c, n_chunks, bias=None, cols=slice(None)):
    seq_len = n_chunks * CHUNK
    r0 = pl.multiple_of(c * CHUNK, CHUNK)
    cur = x_ref[pl.ds(r0, CHUNK), cols]
    lo = pl.multiple_of(jnp.maximum(r0 - SUBLANES, 0), SUBLANES)
    hi = pl.multiple_of(jnp.minimum(r0 + CHUNK, seq_len - SUBLANES), SUBLANES)
    prev = jnp.where(c > 0, x_ref[pl.ds(lo, SUBLANES), cols], 0.0)
    nxt = jnp.where(c < n_chunks - 1, x_ref[pl.ds(hi, SUBLANES), cols], 0.0)
    ext = jnp.concatenate([prev, cur, nxt], axis=0)
    n_ext = CHUNK + 2 * SUBLANES
    pad = CONV_TAPS // 2
    acc = None
    for j in range(CONV_TAPS):
        shifted = ext if j == pad else pltpu.roll(ext, (pad - j) % n_ext, 0)
        term = shifted[SUBLANES:SUBLANES + CHUNK, :] * w[j:j + 1, :]
        acc = term if acc is None else acc + term
    if bias is not None:
        acc = acc + bias
    return acc


def _chunk_cumsums(a):
    tril = (_iota2((CHUNK, CHUNK), 1) <= _iota2((CHUNK, CHUNK), 0)).astype(BF16)
    hi, mid, lo = _split3(a)
    pre = (jnp.dot(tril, hi, preferred_element_type=F32)
           + jnp.dot(tril, mid, preferred_element_type=F32)
           + jnp.dot(tril, lo, preferred_element_type=F32))
    tot = pre[CHUNK - 1:CHUNK, :]
    suf = tot - pre + a
    return pre, suf, tot


def _pick_col(x, lane_ids, c):
    return jnp.sum(jnp.where(lane_ids == c, x, 0.0), axis=1, keepdims=True)


def _pick_row(x, row_ids, r):
    return jnp.sum(jnp.where(row_ids == r, x, 0.0), axis=0, keepdims=True)


def _scan_order_cumsum(a_col, a_row, lower):
    ii = _iota2((CHUNK, CHUNK), 0)
    jj = _iota2((CHUNK, CHUNK), 1)
    tril = (jj <= ii).astype(BF16)
    triu = (jj >= ii).astype(BF16)
    left, right = (tril, triu) if lower else (triu, tril)
    g_col = sum(jnp.dot(left, p, preferred_element_type=F32)
                for p in _split3(jnp.broadcast_to(a_col, (CHUNK, CHUNK))))
    g_row = sum(jnp.dot(p, right, preferred_element_type=F32)
                for p in _split3(jnp.broadcast_to(a_row, (CHUNK, CHUNK))))
    tot = g_col[CHUNK - 1:CHUNK, :] if lower else g_col[0:1, :]
    return g_col, g_row, tot


def _decay_matrix(g_col, g_row, incl):
    d = g_col - g_row
    return jnp.where(incl, jnp.exp(jnp.where(incl, d, 0.0)), 0.0)


def _unit_tri_inverses(lms, lowers):
    ii = _iota2((CHUNK, CHUNK), 0)
    jj = _iota2((CHUNK, CHUNK), 1)
    xs = None
    s = 1
    while s < CHUNK:
        sh = s.bit_length() - 1
        bi = ii >> sh
        bj = jj >> sh
        same = (bi >> 1) == (bj >> 1)
        m_lower = same & ((bi & 1) == 1) & ((bj & 1) == 0)
        m_upper = same & ((bi & 1) == 0) & ((bj & 1) == 1)
        offs = [jnp.where(m_lower if lower else m_upper, lm, 0.0) for lm, lower in zip(lms, lowers)]
        if xs is None:
            eye = (ii == jj).astype(F32)
            xs = [eye - off for off in offs]
        else:
            ps = [_dot(off, x) for off, x in zip(offs, xs)]
            xs = [x - _dot(x, p) for x, p in zip(xs, ps)]
        s *= 2
    return xs


def _gdn_kernel(*refs, n_chunks, has_state, hps):
    (q_ref, k_ref, v_ref, gate_ref, sm_ref, wq_ref, wk_ref, wv_ref, alog_ref, dtb_ref, gn_ref) = refs[:11]
    pos = 11
    if has_state:
        s0_ref = refs[pos]
        pos += 1
    out_ref, sfin_ref = refs[pos:pos + 2]
    (u_scr, w_scr, qk_scr, qg_scr, kg_scr, egl_scr, o_scr) = refs[pos + 2:]
    h0 = pl.program_id(1) * hps
    lane_ids = _iota2((CHUNK, LANES), 1)
    ii = _iota2((CHUNK, CHUNK), 0)
    jj = _iota2((CHUNK, CHUNK), 1)
    neg_a = -jnp.exp(alog_ref[...])
    dtb = dtb_ref[...]
    prep_chunks = int(np.gcd(n_chunks, max(GDN_PREP_CHUNKS // hps, 1)))
    problems = [(d, hh) for d in range(N_DIR) for hh in range(hps)]

    def prep(i, carry):
        chains = []
        for cc in range(prep_chunks):
            c = i * prep_chunks + cc
            r0 = pl.multiple_of(c * CHUNK, CHUNK)
            sm = sm_ref[pl.ds(r0, CHUNK), :]
            beta_all = jax.nn.sigmoid(sm)
            la_all = neg_a * _softplus(sm + dtb)
            la_t = la_all.T
            for hh in range(hps):
                cols = slice(hh * LANES, (hh + 1) * LANES)
                q = _silu(_conv_chunk(q_ref, wq_ref[:, cols], c, n_chunks, cols=cols))
                k = _silu(_conv_chunk(k_ref, wk_ref[:, cols], c, n_chunks, cols=cols))
                v = _silu(_conv_chunk(v_ref, wv_ref[:, cols], c, n_chunks, cols=cols))
                q = q * lax.rsqrt(jnp.sum(q * q, axis=-1, keepdims=True) + EPS) * (GDN_DK ** -0.5)
                k = k * lax.rsqrt(jnp.sum(k * k, axis=-1, keepdims=True) + EPS)
                kk = _dot_nt(k, k)
                qk = _dot_nt(q, k)
                for d in range(N_DIR):
                    lower = d == 0
                    col = SM_DECAY + d * GDN_HEADS + h0 + hh
                    g_col, g_row, g_last = _scan_order_cumsum(
                        _pick_col(la_all, lane_ids, col), _pick_row(la_t, ii, col), lower)
                    beta = _pick_col(beta_all, lane_ids, SM_BETA + d * GDN_HEADS + h0 + hh)
                    incl = (jj <= ii) if lower else (jj >= ii)
                    strict = (jj < ii) if lower else (jj > ii)
                    decay = _decay_matrix(g_col, g_row, incl)
                    chains.append(dict(c=c, r0=r0, d=d, p=d * hps + hh, q=q, k=k, v=v, qk=qk, beta=beta,
                                       decay=decay, g_col=g_col, g_last=g_last,
                                       lm=jnp.where(strict, kk * beta * decay, 0.0)))
        ts = _unit_tri_inverses([ch['lm'] for ch in chains], [ch['d'] == 0 for ch in chains])
        egs = [jnp.exp(ch['g_col']) for ch in chains]
        us = [_dot(t, ch['v'] * ch['beta']) for t, ch in zip(ts, chains)]
        ws = [_dot(t, ch['k'] * (ch['beta'] * eg)) for t, ch, eg in zip(ts, chains, egs)]
        for ch, eg, u, w in zip(chains, egs, us, ws):
            p, rows = ch['p'], pl.ds(ch['r0'], CHUNK)
            u_scr[p, rows, :] = u
            w_scr[p, rows, :] = w.astype(BF16)
            qk_scr[p, rows, :] = (ch['qk'] * ch['decay']).astype(BF16)
            qg_scr[p, rows, :] = (ch['q'] * eg).astype(BF16)
            kg_scr[p, rows, :] = (ch['k'] * jnp.exp(ch['g_last'] - ch['g_col'])).astype(BF16)
            egl_scr[p, pl.ds(ch['c'], 1), :] = jnp.exp(ch['g_last'])
        return carry

    lax.fori_loop(0, n_chunks // prep_chunks, prep, 0)
    o_scr[...] = jnp.zeros_like(o_scr)

    def scan(i, carry):
        cs = (i, n_chunks - 1 - i)
        rows = [pl.ds(pl.multiple_of(c * CHUNK, CHUNK), CHUNK) for c in cs]
        n_p = len(problems)
        sbs = [s.astype(BF16) for s in carry]
        ws = [jnp.dot(w_scr[p, rows[d], :], sbs[p], preferred_element_type=F32)
              for p, (d, _) in enumerate(problems)]
        gs = [jnp.dot(qg_scr[p, rows[d], :], sbs[p], preferred_element_type=F32)
              for p, (d, _) in enumerate(problems)]
        vbs = [(u_scr[p, rows[d], :] - ws[p]).astype(BF16) for p, (d, _) in enumerate(problems)]
        os = [gs[p] + jnp.dot(qk_scr[p, rows[d], :], vbs[p], preferred_element_type=F32)
              for p, (d, _) in enumerate(problems)]
        new = [carry[p] * egl_scr[p, pl.ds(cs[d], 1), :] + _dot_tn(kg_scr[p, rows[d], :], vbs[p])
               for p, (d, _) in enumerate(problems)]
        for p, (d, hh) in enumerate(problems):
            o_scr[hh, rows[d], :] += os[p]
        return tuple(new[:n_p])

    if has_state:
        init = tuple(s0_ref[d, hh] for d, hh in problems)
    else:
        init = tuple(jnp.zeros((GDN_DK, GDN_DV), F32) for _ in problems)
    final = lax.fori_loop(0, n_chunks, scan, init)
    for (d, hh), s_fin in zip(problems, final):
        sfin_ref[d, hh] = s_fin

    def finish(c, carry):
        rows = pl.ds(pl.multiple_of(c * CHUNK, CHUNK), CHUNK)
        for hh in range(hps):
            cols = slice(hh * LANES, (hh + 1) * LANES)
            o = o_scr[hh, rows, :]
            y = o * lax.rsqrt(jnp.mean(o * o, axis=-1, keepdims=True) + EPS) * gn_ref[...]
            out_ref[rows, cols] = (y * _silu(gate_ref[rows, cols])).astype(out_ref.dtype)
        return carry

    lax.fori_loop(0, n_chunks, finish, 0)


def gated_deltanet(proj, small, conv_w, alog_vec, dtb_vec, gnorm, n_seq, seq_len, state=None, hps=1):
    n_tok = n_seq * seq_len
    n_chunks = seq_len // CHUNK
    has_state = state is not None
    nb = GDN_HEADS // hps
    bw = hps * LANES
    in_specs = [
        pl.BlockSpec((seq_len, bw), lambda b, h: (b, h)),
        pl.BlockSpec((seq_len, bw), lambda b, h: (b, nb + h)),
        pl.BlockSpec((seq_len, bw), lambda b, h: (b, 2 * nb + h)),
        pl.BlockSpec((seq_len, bw), lambda b, h: (b, 3 * nb + h)),
        pl.BlockSpec((seq_len, LANES), lambda b, h: (b, 0)),
        pl.BlockSpec((CONV_TAPS, bw), lambda b, h: (0, h)),
        pl.BlockSpec((CONV_TAPS, bw), lambda b, h: (0, nb + h)),
        pl.BlockSpec((CONV_TAPS, bw), lambda b, h: (0, 2 * nb + h)),
        pl.BlockSpec((1, LANES), lambda b, h: (0, 0)),
        pl.BlockSpec((1, LANES), lambda b, h: (0, 0)),
        pl.BlockSpec((1, LANES), lambda b, h: (0, 0)),
    ]
    args = [proj, proj, proj, proj, small, conv_w, conv_w, conv_w, alog_vec, dtb_vec,
            gnorm.reshape(1, GDN_DV)]
    if has_state:
        state_gdn, layer = state
        in_specs.append(pl.BlockSpec((None, None, N_DIR, hps, GDN_DK, GDN_DV),
                                     lambda b, h: (b, layer, 0, h, 0, 0)))
        args.append(state_gdn)
    n_prob = N_DIR * hps
    return pl.pallas_call(
        functools.partial(_gdn_kernel, n_chunks=n_chunks, has_state=has_state, hps=hps),
        grid=(n_seq, nb),
        in_specs=in_specs,
        out_specs=[
            pl.BlockSpec((seq_len, bw), lambda b, h: (b, h)),
            pl.BlockSpec((None, N_DIR, hps, GDN_DK, GDN_DV), lambda b, h: (b, 0, h, 0, 0)),
        ],
        out_shape=[jax.ShapeDtypeStruct((n_tok, GDN_WIDTH), BF16),
                   jax.ShapeDtypeStruct((n_seq, N_DIR, GDN_HEADS, GDN_DK, GDN_DV), F32)],
        scratch_shapes=[
            pltpu.VMEM((n_prob, seq_len, GDN_DV), F32),
            pltpu.VMEM((n_prob, seq_len, GDN_DK), BF16),
            pltpu.VMEM((n_prob, seq_len, CHUNK), BF16),
            pltpu.VMEM((n_prob, seq_len, GDN_DK), BF16),
            pltpu.VMEM((n_prob, seq_len, GDN_DK), BF16),
            pltpu.VMEM((n_prob, max(n_chunks, SUBLANES), LANES), F32),
            pltpu.VMEM((hps, seq_len, GDN_DV), F32),
        ],
        compiler_params=_params("parallel", "parallel"),
        name="gdn_latent" if has_state else "gdn_context",
    )(*args)


def _ssd_kernel(*refs, n_chunks, has_state):
    (x_ref, b_ref, c_ref, z_ref, sm_ref, wx_ref, wb_ref, wc_ref, bx_ref, bb_ref, bc_ref,
     alog_ref, dtb_ref, dvec_ref, gn_ref) = refs[:15]
    pos = 15
    if has_state:
        h0_ref = refs[pos]
        pos += 1
    out_ref, hfin_ref = refs[pos:pos + 2]
    xs_scr, bs_scr, cs_scr, y_scr, h_scr = refs[pos + 2:]
    g = pl.program_id(1)
    hpg = SSD_HEADS_PER_GROUP
    P = SSD_HEAD_DIM
    lane_ids = _iota2((CHUNK, LANES), 1)
    lane_row = _iota2((1, LANES), 1)
    ii = _iota2((CHUNK, CHUNK), 0)
    jj = _iota2((CHUNK, CHUNK), 1)
    neg_a = -jnp.exp(alog_ref[...])
    dtb = dtb_ref[...]

    def prep(c, carry):
        r0 = pl.multiple_of(c * CHUNK, CHUNK)
        xs_scr[pl.ds(r0, CHUNK), :] = _silu(_conv_chunk(x_ref, wx_ref[...], c, n_chunks, bx_ref[...]))
        bs_scr[pl.ds(r0, CHUNK), :] = _silu(_conv_chunk(b_ref, wb_ref[...], c, n_chunks, bb_ref[...]))
        cs_scr[pl.ds(r0, CHUNK), :] = _silu(_conv_chunk(c_ref, wc_ref[...], c, n_chunks, bc_ref[...]))
        return carry

    lax.fori_loop(0, n_chunks, prep, 0)
    y_scr[...] = jnp.zeros_like(y_scr)
    if has_state:
        h_scr[...] = h0_ref[...]
    else:
        h_scr[...] = jnp.zeros_like(h_scr)

    def scan(i, carry):
        work = []
        for d, c in ((0, i), (1, n_chunks - 1 - i)):
            rows = pl.ds(pl.multiple_of(c * CHUNK, CHUNK), CHUNK)
            lower = d == 0
            x = xs_scr[rows, :]
            bm = bs_scr[rows, :]
            cm = cs_scr[rows, :]
            dt_all = _softplus(sm_ref[rows, :] + dtb)
            pre, suf, tot = _chunk_cumsums(dt_all * neg_a)
            ac = pre if lower else suf
            ac_t = ac.T
            cb = _dot_nt(cm, bm)
            incl = (jj <= ii) if lower else (jj >= ii)
            for hh in range(hpg):
                col = SM_DT + d * SSD_HEADS + g * hpg + hh
                a_col = _pick_col(ac, lane_ids, col)
                a_last = jnp.sum(jnp.where(lane_row == col, tot, 0.0), axis=1, keepdims=True)
                work.append(dict(
                    d=d, hh=hh, rows=rows,
                    xdt=x[:, hh * P:(hh + 1) * P] * _pick_col(dt_all, lane_ids, col),
                    cbd=cb * _decay_matrix(a_col, _pick_row(ac_t, ii, col), incl),
                    b_dec=bm * jnp.exp(a_last - a_col),
                    c_dec=cm * jnp.exp(a_col),
                    e_last=jnp.exp(a_last)))
        y_in = [_dot(w['cbd'], w['xdt']) for w in work]
        sts = [_dot_tn(w['xdt'], w['b_dec']) for w in work]
        y_off = [_dot_nt(w['c_dec'], h_scr[w['d'], w['hh']]) for w in work]
        for w, yi, st, yo in zip(work, y_in, sts, y_off):
            d, hh = w['d'], w['hh']
            h_scr[d, hh] = h_scr[d, hh] * w['e_last'] + st
            y_scr[w['rows'], hh * P:(hh + 1) * P] += yi + yo
        return carry

    lax.fori_loop(0, n_chunks, scan, 0)
    hfin_ref[...] = h_scr[...]

    def finish(c, carry):
        r0 = pl.multiple_of(c * CHUNK, CHUNK)
        y = y_scr[pl.ds(r0, CHUNK), :] + dvec_ref[...] * xs_scr[pl.ds(r0, CHUNK), :]
        y = y * _silu(z_ref[pl.ds(r0, CHUNK), :])
        y = y * lax.rsqrt(jnp.mean(y * y, axis=-1, keepdims=True) + EPS) * gn_ref[...]
        out_ref[pl.ds(r0, CHUNK), :] = y.astype(out_ref.dtype)
        return carry

    lax.fori_loop(0, n_chunks, finish, 0)


def ssd_mixer(proj, small, conv_w, conv_b, alog_vec, dtb_vec, d_vec, gnorm, n_seq, seq_len, state=None):
    n_tok = n_seq * seq_len
    n_chunks = seq_len // CHUNK
    has_state = state is not None
    gw = SSD_GROUP_WIDTH
    hpg = SSD_HEADS_PER_GROUP
    xb = OFF_SXBC // gw
    bb = (OFF_SXBC + SSD_WIDTH) // LANES
    cb = bb + SSD_GROUPS
    zb = OFF_SZ // gw
    in_specs = [
        pl.BlockSpec((seq_len, gw), lambda b, g: (b, xb + g)),
        pl.BlockSpec((seq_len, LANES), lambda b, g: (b, bb + g)),
        pl.BlockSpec((seq_len, LANES), lambda b, g: (b, cb + g)),
        pl.BlockSpec((seq_len, gw), lambda b, g: (b, zb + g)),
        pl.BlockSpec((seq_len, LANES), lambda b, g: (b, 0)),
        pl.BlockSpec((CONV_TAPS, gw), lambda b, g: (0, g)),
        pl.BlockSpec((CONV_TAPS, LANES), lambda b, g: (0, SSD_WIDTH // LANES + g)),
        pl.BlockSpec((CONV_TAPS, LANES), lambda b, g: (0, SSD_WIDTH // LANES + SSD_GROUPS + g)),
        pl.BlockSpec((1, gw), lambda b, g: (0, g)),
        pl.BlockSpec((1, LANES), lambda b, g: (0, SSD_WIDTH // LANES + g)),
        pl.BlockSpec((1, LANES), lambda b, g: (0, SSD_WIDTH // LANES + SSD_GROUPS + g)),
        pl.BlockSpec((1, LANES), lambda b, g: (0, 0)),
        pl.BlockSpec((1, LANES), lambda b, g: (0, 0)),
        pl.BlockSpec((1, gw), lambda b, g: (0, g)),
        pl.BlockSpec((1, gw), lambda b, g: (0, g)),
    ]
    conv_b2 = conv_b.reshape(1, -1)
    args = [proj, proj, proj, proj, small, conv_w, conv_w, conv_w, conv_b2, conv_b2, conv_b2,
            alog_vec, dtb_vec, d_vec, gnorm.reshape(1, SSD_WIDTH)]
    if has_state:
        state_ssd, layer = state
        in_specs.append(pl.BlockSpec((None, None, N_DIR, hpg, SSD_HEAD_DIM, SSD_STATE),
                                     lambda b, g: (b, layer, 0, g, 0, 0)))
        args.append(state_ssd)
    return pl.pallas_call(
        functools.partial(_ssd_kernel, n_chunks=n_chunks, has_state=has_state),
        grid=(n_seq, SSD_GROUPS),
        in_specs=in_specs,
        out_specs=[
            pl.BlockSpec((seq_len, gw), lambda b, g: (b, g)),
            pl.BlockSpec((None, N_DIR, hpg, SSD_HEAD_DIM, SSD_STATE), lambda b, g: (b, 0, g, 0, 0)),
        ],
        out_shape=[jax.ShapeDtypeStruct((n_tok, SSD_WIDTH), BF16),
                   jax.ShapeDtypeStruct((n_seq, N_DIR, SSD_HEADS, SSD_HEAD_DIM, SSD_STATE), F32)],
        scratch_shapes=[
            pltpu.VMEM((seq_len, gw), F32),
            pltpu.VMEM((seq_len, SSD_STATE), F32),
            pltpu.VMEM((seq_len, SSD_STATE), F32),
            pltpu.VMEM((seq_len, gw), F32),
            pltpu.VMEM((N_DIR, hpg, SSD_HEAD_DIM, SSD_STATE), F32),
        ],
        compiler_params=_params("parallel", "parallel"),
        name="ssd_latent" if has_state else "ssd_context",
    )(*args)


def _top2_sum(a, b, c, d):
    hi1, lo1 = jnp.maximum(a, b), jnp.minimum(a, b)
    hi2, lo2 = jnp.maximum(c, d), jnp.minimum(c, d)
    return jnp.maximum(hi1, hi2) + jnp.maximum(jnp.minimum(hi1, hi2), jnp.maximum(lo1, lo2))


def _outproj_kernel(x_ref, gdn_ref, att_ref, ssd_ref, mod_ref, g_ref, w_ref, wr_ref, br_ref,
                    xo_ref, h_ref, route_ref):
    mixed = (jnp.dot(gdn_ref[...], w_ref[0:GDN_WIDTH, :], preferred_element_type=F32)
             + jnp.dot(att_ref[...], w_ref[GDN_WIDTH:GDN_WIDTH + ATT_WIDTH, :], preferred_element_type=F32)
             + jnp.dot(ssd_ref[...], w_ref[GDN_WIDTH + ATT_WIDTH:, :], preferred_element_type=F32))
    x = x_ref[...] + mod_ref[2:3, :] * mixed
    xo_ref[...] = x
    y = x * lax.rsqrt(jnp.mean(x * x, axis=-1, keepdims=True) + EPS)
    h = (y * g_ref[...]) * (1.0 + mod_ref[4:5, :]) + mod_ref[3:4, :]
    h_ref[...] = h
    lt = _dot_nt(wr_ref[...], h)
    sc = [jax.nn.sigmoid(lt[e:e + 1, :]) for e in range(N_EXPERTS)]
    sel = [sc[e] + br_ref[e:e + 1, :] for e in range(N_EXPERTS)]
    epg = EXPERTS_PER_GROUP
    gs = [_top2_sum(*sel[gi * epg:(gi + 1) * epg]) for gi in range(N_EXPERT_GROUPS)]
    best, bg = gs[0], jnp.zeros_like(gs[0], dtype=jnp.int32)
    for gi in range(1, N_EXPERT_GROUPS):
        upd = gs[gi] > best
        bg = jnp.where(upd, gi, bg)
        best = jnp.where(upd, gs[gi], best)
    cand_sel, cand_sc = [], []
    for k in range(epg):
        vs, vc = sel[k], sc[k]
        for gi in range(1, N_EXPERT_GROUPS):
            vs = jnp.where(bg == gi, sel[gi * epg + k], vs)
            vc = jnp.where(bg == gi, sc[gi * epg + k], vc)
        cand_sel.append(vs)
        cand_sc.append(vc)
    m1, i1, g1 = cand_sel[0], jnp.zeros_like(bg), cand_sc[0]
    for k in range(1, epg):
        upd = cand_sel[k] > m1
        i1 = jnp.where(upd, k, i1)
        g1 = jnp.where(upd, cand_sc[k], g1)
        m1 = jnp.where(upd, cand_sel[k], m1)
    m2 = jnp.full_like(m1, -jnp.inf)
    i2, g2 = jnp.full_like(bg, -1), jnp.zeros_like(g1)
    for k in range(epg):
        upd = (i1 != k) & ((cand_sel[k] > m2) | (i2 < 0))
        i2 = jnp.where(upd, k, i2)
        g2 = jnp.where(upd, cand_sc[k], g2)
        m2 = jnp.where(upd, cand_sel[k], m2)
    den = g1 + g2
    route_ref[...] = jnp.zeros_like(route_ref)
    route_ref[0:1, :] = (bg * epg + i1).astype(F32)
    route_ref[1:2, :] = (bg * epg + i2).astype(F32)
    route_ref[2:3, :] = g1 / den
    route_ref[3:4, :] = g2 / den


def out_projection(x, gdn, att, ssd, mod, mod_row, g, w_out, w_router, b_router, tm=512):
    n_tok = x.shape[0]
    return pl.pallas_call(
        _outproj_kernel,
        grid=(n_tok // tm,),
        in_specs=[
            pl.BlockSpec((tm, D_MODEL), lambda i: (i, 0)),
            pl.BlockSpec((tm, GDN_WIDTH), lambda i: (i, 0)),
            pl.BlockSpec((tm, ATT_WIDTH), lambda i: (i, 0)),
            pl.BlockSpec((tm, SSD_WIDTH), lambda i: (i, 0)),
            pl.BlockSpec((None, 6, D_MODEL), lambda i: (mod_row(tm)(i), 0, 0)),
            pl.BlockSpec((1, D_MODEL), lambda i: (0, 0)),
            pl.BlockSpec((MIX_WIDTH, D_MODEL), lambda i: (0, 0)),
            pl.BlockSpec((LANES, D_MODEL), lambda i: (0, 0)),
            pl.BlockSpec((LANES, 1), lambda i: (0, 0)),
        ],
        out_specs=[
            pl.BlockSpec((tm, D_MODEL), lambda i: (i, 0)),
            pl.BlockSpec((tm, D_MODEL), lambda i: (i, 0)),
            pl.BlockSpec((SUBLANES, tm), lambda i: (0, i)),
        ],
        out_shape=[jax.ShapeDtypeStruct((n_tok, D_MODEL), F32),
                   jax.ShapeDtypeStruct((n_tok, D_MODEL), F32),
                   jax.ShapeDtypeStruct((SUBLANES, n_tok), F32)],
        compiler_params=_params("parallel"),
        name="out_projection",
    )(x, gdn, att, ssd, mod, g, w_out, w_router, b_router)


def _expert_kernel(idx_ref, be_ref, nb_ref, h_ref, wg_ref, wu_ref, wd_ref, o_ref, xbuf, sem, *, tm):
    i = pl.program_id(0)
    n_used = nb_ref[0]

    def row_copy(step, slot, r):
        return pltpu.make_async_copy(h_ref.at[pl.ds(idx_ref[step * tm + r], 1)],
                                     xbuf.at[slot, pl.ds(r, 1)], sem.at[slot])

    def start_rows(step, slot):
        def body(r2, carry):
            row_copy(step, slot, 2 * r2).start(priority=0)
            row_copy(step, slot, 2 * r2 + 1).start(priority=1)
            return carry
        lax.fori_loop(0, tm // 2, body, 0, unroll=4)

    def wait_rows(step, slot):
        def body(r, carry):
            row_copy(step, slot, r).wait()
            return carry
        lax.fori_loop(0, tm, body, 0, unroll=8)

    @pl.when(i == 0)
    def _():
        start_rows(0, 0)

    @pl.when(i + 1 < n_used)
    def _():
        start_rows(i + 1, (i + 1) % 2)

    @pl.when(i < n_used)
    def _():
        slot = i % 2
        wait_rows(i, slot)
        xb = xbuf[slot].astype(BF16)
        a = jnp.dot(xb, wg_ref[...], preferred_element_type=F32)
        u = jnp.dot(xb, wu_ref[...], preferred_element_type=F32)
        o_ref[...] = jnp.dot((_silu(a) * u).astype(BF16), wd_ref[...], preferred_element_type=F32)

    @pl.when(i >= n_used)
    def _():
        o_ref[...] = jnp.zeros_like(o_ref)


def _cast_kernel(x_ref, o_ref):
    o_ref[...] = x_ref[...].astype(o_ref.dtype)


def cast_bf16(x):
    n, r, c = x.shape
    return pl.pallas_call(
        _cast_kernel,
        grid=(n,),
        in_specs=[pl.BlockSpec((None, r, c), lambda i: (i, 0, 0))],
        out_specs=pl.BlockSpec((None, r, c), lambda i: (i, 0, 0)),
        out_shape=jax.ShapeDtypeStruct((n, r, c), BF16),
        compiler_params=_params("parallel"),
        name="cast_bf16",
    )(x)


def expert_mlp(row_tok, block_e, n_used, h, w_gate, w_up, w_down, tm, layer):
    n_rows = row_tok.shape[0]
    e0 = layer * N_EXPERTS
    return pl.pallas_call(
        functools.partial(_expert_kernel, tm=tm),
        grid_spec=pltpu.PrefetchScalarGridSpec(
            num_scalar_prefetch=3,
            grid=(n_rows // tm,),
            in_specs=[
                pl.BlockSpec(memory_space=pl.ANY),
                pl.BlockSpec((None, D_MODEL, EXPERT_FF), lambda i, idx, be, nb: (e0 + be[i], 0, 0)),
                pl.BlockSpec((None, D_MODEL, EXPERT_FF), lambda i, idx, be, nb: (e0 + be[i], 0, 0)),
                pl.BlockSpec((None, EXPERT_FF, D_MODEL), lambda i, idx, be, nb: (e0 + be[i], 0, 0)),
            ],
            out_specs=pl.BlockSpec((tm, D_MODEL), lambda i, idx, be, nb: (i, 0)),
            scratch_shapes=[pltpu.VMEM((2, tm, D_MODEL), F32), pltpu.SemaphoreType.DMA((2,))],
        ),
        out_shape=jax.ShapeDtypeStruct((n_rows, D_MODEL), F32),
        compiler_params=_params("arbitrary"),
        name="expert_mlp",
    )(row_tok, block_e, n_used, h, w_gate, w_up, w_down)


def _combine_kernel(dest_ref, x_ref, gates_ref, mod_ref, y_ref, o_ref, buf, sem, *, tm, n_tok):
    i = pl.program_id(0)

    def copy(step, slot, k, r):
        return pltpu.make_async_copy(y_ref.at[pl.ds(dest_ref[k * n_tok + step * tm + r], 1)],
                                     buf.at[slot, k, pl.ds(r, 1)], sem.at[slot])

    def start_rows(step, slot):
        def body(r, carry):
            for k in range(TOP_K):
                copy(step, slot, k, r).start(priority=k)
            return carry
        lax.fori_loop(0, tm, body, 0, unroll=4)

    def wait_rows(step, slot):
        def body(r, carry):
            for k in range(TOP_K):
                copy(step, slot, k, r).wait()
            return carry
        lax.fori_loop(0, tm, body, 0, unroll=4)

    @pl.when(i == 0)
    def _():
        start_rows(0, 0)

    @pl.when(i + 1 < pl.num_programs(0))
    def _():
        start_rows(i + 1, (i + 1) % 2)

    slot = i % 2
    wait_rows(i, slot)
    moe = gates_ref[:, 0:1] * buf[slot, 0] + gates_ref[:, 1:2] * buf[slot, 1]
    o_ref[...] = x_ref[...] + mod_ref[5:6, :] * moe


def moe_combine(x, y_rows, dest, gates, mod, mod_row, tm=256):
    n_tok = x.shape[0]
    return pl.pallas_call(
        functools.partial(_combine_kernel, tm=tm, n_tok=n_tok),
        grid_spec=pltpu.PrefetchScalarGridSpec(
            num_scalar_prefetch=1,
            grid=(n_tok // tm,),
            in_specs=[
                pl.BlockSpec((tm, D_MODEL), lambda i, d: (i, 0)),
                pl.BlockSpec((tm, TOP_K), lambda i, d: (i, 0)),
                pl.BlockSpec((None, 6, D_MODEL), lambda i, d: (mod_row(tm)(i), 0, 0)),
                pl.BlockSpec(memory_space=pl.ANY),
            ],
            out_specs=pl.BlockSpec((tm, D_MODEL), lambda i, d: (i, 0)),
            scratch_shapes=[pltpu.VMEM((2, TOP_K, tm, D_MODEL), F32), pltpu.SemaphoreType.DMA((2,))],
        ),
        out_shape=jax.ShapeDtypeStruct((n_tok, D_MODEL), F32),
        compiler_params=_params("arbitrary"),
        name="moe_combine",
    )(dest, x, gates, mod, y_rows)


def dispatch_plan(route, tm):
    n_tok = route.shape[1]
    e_idx = route[0:TOP_K].astype(jnp.int32)
    flat_e = e_idx.reshape(-1)
    n_assign = flat_e.shape[0]
    onehot = (flat_e[:, None] == jnp.arange(N_EXPERTS, dtype=jnp.int32)[None, :]).astype(jnp.int32)
    csum = jnp.cumsum(onehot, axis=0)
    rank = jnp.sum(onehot * csum, axis=1) - 1
    counts = csum[-1]
    padded = (counts + tm - 1) // tm * tm
    padded_end = jnp.cumsum(padded)
    padded_start = padded_end - padded
    dest = padded_start[flat_e] + rank
    n_blocks = n_assign // tm + N_EXPERTS
    n_rows = n_blocks * tm
    tok = jnp.tile(jnp.arange(n_tok, dtype=jnp.int32), TOP_K)
    row_tok = jnp.zeros((n_rows,), jnp.int32).at[dest].set(tok)
    block_e = jnp.minimum(
        jnp.searchsorted(padded_end, jnp.arange(n_blocks, dtype=jnp.int32) * tm, side='right'),
        N_EXPERTS - 1).astype(jnp.int32)
    gates = route[TOP_K:2 * TOP_K].T
    n_used = (padded_end[-1:] // tm).astype(jnp.int32)
    return row_tok, block_e, n_used, dest, gates


def _lane_vec(pairs):
    v = jnp.zeros((LANES,), F32)
    for off, arr in pairs:
        v = lax.dynamic_update_slice(v, arr.reshape(-1).astype(F32), (off,))
    return v.reshape(1, LANES)


def _rope_tables(n_tok):
    n_rows = n_tok // GRID_W
    rows = np.repeat(np.arange(n_rows), GRID_W).astype(np.float32)
    cols = np.tile(np.arange(GRID_W), n_rows).astype(np.float32)
    inv = 1.0 / (ROPE_THETA ** (jnp.arange(ROPE_AXIS_PAIRS, dtype=F32) / ROPE_AXIS_PAIRS))
    ang = jnp.concatenate([rows[:, None] * inv, cols[:, None] * inv], axis=-1)
    cos, sin = jnp.cos(ang), jnp.sin(ang)
    return jnp.concatenate([cos, cos], axis=-1), jnp.concatenate([-sin, sin], axis=-1)


def _reorder_w_in(w):
    c0 = 4 * GDN_WIDTH
    c1 = c0 + 2 * N_DIR * GDN_HEADS
    c2 = w.shape[1] - N_DIR * SSD_HEADS
    main = jnp.concatenate([w[:, :c0], w[:, c1:c2]], axis=1)
    small = jnp.concatenate([w[:, c0:c1], w[:, c2:],
                             jnp.zeros((w.shape[0], LANES - SM_USED), w.dtype)], axis=1)
    return main.astype(BF16), small.astype(BF16)


def _group_layer(x, mod, mod_row, lp, shared, n_seq, seq_len, tm_in, tq, moe_tm, ctx):
    proj, small = in_projection(x, mod, mod_row, lp['norm_mix'], lp['w_in_main'], lp['w_in_small'], tm_in)
    if ctx is None:
        att, k_new, v_new = attention(proj, lp['q_norm'], lp['k_norm'], n_seq, seq_len, tq)
        gdn, s_gdn = gated_deltanet(proj, small, lp['gdn_conv'], lp['gdn_alog_vec'], lp['gdn_dtb_vec'],
                                    lp['gdn_norm'], n_seq, seq_len, hps=GDN_HEADS)
        ssd, s_ssd = ssd_mixer(proj, small, lp['ssd_conv_w'], lp['ssd_conv_b'], lp['ssd_alog_vec'],
                               lp['ssd_dtb_vec'], lp['ssd_d_vec'], lp['ssd_norm'], n_seq, seq_len)
        new_ctx = (k_new, v_new, s_gdn, s_ssd)
    else:
        cache_k4, cache_v4, state_gdn, state_ssd, layer, cos2, sin2 = ctx
        att = attention(proj, lp['q_norm'], lp['k_norm'], n_seq, seq_len, tq,
                        ctx=(cache_k4, cache_v4, layer, cos2, sin2))[0]
        gdn, _ = gated_deltanet(proj, small, lp['gdn_conv'], lp['gdn_alog_vec'], lp['gdn_dtb_vec'],
                                lp['gdn_norm'], n_seq, seq_len, state=(state_gdn, layer))
        ssd, _ = ssd_mixer(proj, small, lp['ssd_conv_w'], lp['ssd_conv_b'], lp['ssd_alog_vec'],
                           lp['ssd_dtb_vec'], lp['ssd_d_vec'], lp['ssd_norm'], n_seq, seq_len,
                           state=(state_ssd, layer))
        new_ctx = None
    x_mid, h2, route = out_projection(x, gdn, att, ssd, mod, mod_row, lp['norm_ffn'], lp['w_out'],
                                      shared['w_router'], shared['b_router'])
    row_tok, block_e, n_used, dest, gates = dispatch_plan(route, moe_tm)
    y_rows = expert_mlp(row_tok, block_e, n_used, h2, shared['w_gate'], shared['w_up'], shared['w_down'],
                        moe_tm, lp['layer'])
    x_out = moe_combine(x_mid, y_rows, dest, gates, mod, mod_row)
    return x_out, new_ctx


def kernel(x_prompt, x_sample, c, cache_k, cache_v, state_gdn, state_ssd, c_ctx, w_ada, b_ada, norm_mix,
           norm_ffn, w_in, gdn_conv, gdn_a_log, gdn_dt_bias, gdn_norm, q_norm, k_norm, ssd_conv_w,
           ssd_conv_b, ssd_a_log, ssd_dt_bias, ssd_d, ssd_norm, w_out, w_router, b_router, w_gate, w_up,
           w_down):
    n_ctx, ctx_len, _ = x_prompt.shape
    n_lat, lat_len, _ = x_sample.shape
    past = cache_k.shape[2]
    kv_w = ATT_KV_HEADS * HEAD_DIM

    cond = jnp.concatenate([c_ctx[None, :], c, jnp.zeros((SUBLANES - 1 - n_lat, D_MODEL), F32)], axis=0)
    mod_all = ada_modulation(cond, w_ada, b_ada).reshape(DEPTH, SUBLANES, 6, D_MODEL)

    shared = {
        'w_router': jnp.concatenate([w_router.T, jnp.zeros((LANES - N_EXPERTS, D_MODEL), F32)],
                                    axis=0).astype(BF16),
        'b_router': jnp.concatenate([b_router, jnp.zeros((LANES - N_EXPERTS,), F32)]).reshape(LANES, 1),
    }
    cos2, sin2 = _rope_tables(lat_len)
    cache_k4 = cache_k.reshape(n_lat, DEPTH, past, kv_w)
    cache_v4 = cache_v.reshape(n_lat, DEPTH, past, kv_w)

    shared['w_gate'] = cast_bf16(w_gate.reshape(DEPTH * N_EXPERTS, D_MODEL, EXPERT_FF))
    shared['w_up'] = cast_bf16(w_up.reshape(DEPTH * N_EXPERTS, D_MODEL, EXPERT_FF))
    shared['w_down'] = cast_bf16(w_down.reshape(DEPTH * N_EXPERTS, EXPERT_FF, D_MODEL))
    w_out_b = cast_bf16(w_out)
    tm_ctx = 1024
    tm_lat = 1024

    y_ctx = x_prompt.reshape(n_ctx * ctx_len, D_MODEL)
    y_lat = x_sample.reshape(n_lat * lat_len, D_MODEL)
    new_k, new_v, new_gdn, new_ssd = [], [], [], []
    for l in range(DEPTH):
        w_main, w_small = _reorder_w_in(w_in[l])
        lp = {
            'norm_mix': norm_mix[l].reshape(1, D_MODEL), 'norm_ffn': norm_ffn[l].reshape(1, D_MODEL),
            'w_in_main': w_main, 'w_in_small': w_small,
            'gdn_conv': gdn_conv[l],
            'gdn_alog_vec': _lane_vec([(SM_DECAY, gdn_a_log[l])]),
            'gdn_dtb_vec': _lane_vec([(SM_DECAY, gdn_dt_bias[l])]),
            'gdn_norm': gdn_norm[l], 'q_norm': q_norm[l], 'k_norm': k_norm[l],
            'ssd_conv_w': ssd_conv_w[l], 'ssd_conv_b': ssd_conv_b[l],
            'ssd_alog_vec': _lane_vec([(SM_DT, ssd_a_log[l])]),
            'ssd_dtb_vec': _lane_vec([(SM_DT, ssd_dt_bias[l])]),
            'ssd_d_vec': jnp.repeat(ssd_d[l], SSD_HEAD_DIM).reshape(1, SSD_WIDTH),
            'ssd_norm': ssd_norm[l],
            'w_out': w_out_b[l], 'layer': l,
        }
        mod = mod_all[l]
        y_ctx, (k_l, v_l, g_l, s_l) = _group_layer(
            y_ctx, mod, lambda tm: (lambda i: 0), lp, shared, n_ctx, ctx_len, tm_ctx, ctx_len, 256, None)
        new_k.append(k_l.reshape(n_ctx, ctx_len, ATT_KV_HEADS, HEAD_DIM))
        new_v.append(v_l.reshape(n_ctx, ctx_len, ATT_KV_HEADS, HEAD_DIM))
        new_gdn.append(g_l)
        new_ssd.append(s_l)
        ctx = (cache_k4, cache_v4, state_gdn, state_ssd, l, cos2, sin2)
        y_lat, _ = _group_layer(
            y_lat, mod, lambda tm: (lambda i: 1 + i // (lat_len // tm)), lp, shared, n_lat, lat_len,
            tm_lat, 256, 256, ctx)
    return (y_ctx.reshape(n_ctx, ctx_len, D_MODEL), y_lat.reshape(n_lat, lat_len, D_MODEL),
            jnp.stack(new_k, axis=1), jnp.stack(new_v, axis=1),
            jnp.stack(new_gdn, axis=1), jnp.stack(new_ssd, axis=1))
```

```python
import functools

import numpy as np
import jax
import jax.numpy as jnp
from jax import lax
from jax.experimental import pallas as pl
from jax.experimental.pallas import tpu as pltpu

F32 = jnp.float32
BF16 = jnp.bfloat16

D_MODEL = 2048
DEPTH = 2
GRID_W = 64
N_DIR = 2
EPS = 1e-6
CONV_TAPS = 5
GDN_HEADS = 4
GDN_DK = 128
GDN_DV = 128
GDN_WIDTH = GDN_HEADS * GDN_DV
ATT_HEADS = 8
ATT_KV_HEADS = 2
HEAD_DIM = 128
ATT_GROUP = ATT_HEADS // ATT_KV_HEADS
ATT_WIDTH = ATT_HEADS * HEAD_DIM
ROPE_THETA = 10000.0
ROPE_AXIS_PAIRS = HEAD_DIM // 4
SSD_HEADS = 8
SSD_HEAD_DIM = 64
SSD_WIDTH = SSD_HEADS * SSD_HEAD_DIM
SSD_GROUPS = 2
SSD_HEADS_PER_GROUP = SSD_HEADS // SSD_GROUPS
SSD_GROUP_WIDTH = SSD_WIDTH // SSD_GROUPS
SSD_STATE = 128
MIX_WIDTH = GDN_WIDTH + ATT_WIDTH + SSD_WIDTH
N_EXPERTS = 16
N_EXPERT_GROUPS = 4
EXPERTS_PER_GROUP = N_EXPERTS // N_EXPERT_GROUPS
TOP_K = 2
EXPERT_FF = 1024

LANES = 128
SUBLANES = 8
CHUNK = 128
GDN_PREP_CHUNKS = 8
VMEM_LIMIT = 56 * 1024 * 1024

OFF_GQKV = 0
OFF_GGATE = 3 * GDN_WIDTH
OFF_AQ = OFF_GGATE + GDN_WIDTH
OFF_AK = OFF_AQ + ATT_WIDTH
OFF_AV = OFF_AK + ATT_KV_HEADS * HEAD_DIM
OFF_SZ = OFF_AV + ATT_KV_HEADS * HEAD_DIM
OFF_SXBC = OFF_SZ + SSD_WIDTH
MAIN_COLS = OFF_SXBC + SSD_WIDTH + 2 * SSD_GROUPS * SSD_STATE
SM_BETA = 0
SM_DECAY = SM_BETA + N_DIR * GDN_HEADS
SM_DT = SM_DECAY + N_DIR * GDN_HEADS
SM_USED = SM_DT + N_DIR * SSD_HEADS


def _silu(x):
    return x * jax.nn.sigmoid(x)


def _softplus(x):
    return jnp.maximum(x, 0.0) + jnp.log1p(jnp.exp(-jnp.abs(x)))


def _dot(a, b):
    return jnp.dot(a.astype(BF16), b.astype(BF16), preferred_element_type=F32)


def _dot_nt(a, b):
    return lax.dot_general(a.astype(BF16), b.astype(BF16), (((1,), (1,)), ((), ())),
                           preferred_element_type=F32)


def _dot_tn(a, b):
    return lax.dot_general(a.astype(BF16), b.astype(BF16), (((0,), (0,)), ((), ())),
                           preferred_element_type=F32)


def _split3(x):
    hi = x.astype(BF16)
    r1 = x - hi.astype(F32)
    mid = r1.astype(BF16)
    lo = (r1 - mid.astype(F32)).astype(BF16)
    return hi, mid, lo


def _params(*sem):
    return pltpu.CompilerParams(dimension_semantics=sem, vmem_limit_bytes=VMEM_LIMIT)


def _mod_kernel(c_ref, w_ref, b_ref, o_ref):
    s = _silu(c_ref[...])
    o_ref[...] = _dot(s, w_ref[...]) + b_ref[...]


def ada_modulation(cond, w_ada, b_ada, tn=1024):
    n_out = 6 * D_MODEL
    return pl.pallas_call(
        _mod_kernel,
        grid=(DEPTH, n_out // tn),
        in_specs=[
            pl.BlockSpec((SUBLANES, D_MODEL), lambda l, j: (0, 0)),
            pl.BlockSpec((None, D_MODEL, tn), lambda l, j: (l, 0, j)),
            pl.BlockSpec((None, 1, tn), lambda l, j: (l, 0, j)),
        ],
        out_specs=pl.BlockSpec((None, SUBLANES, tn), lambda l, j: (l, 0, j)),
        out_shape=jax.ShapeDtypeStruct((DEPTH, SUBLANES, n_out), F32),
        compiler_params=_params("parallel", "parallel"),
        name="ada_modulation",
    )(cond, w_ada, b_ada.reshape(DEPTH, 1, n_out))


def _inproj_kernel(x_ref, mod_ref, g_ref, w_ref, ws_ref, o_ref, os_ref, h_scr):
    @pl.when(pl.program_id(1) == 0)
    def _():
        x = x_ref[...]
        y = x * lax.rsqrt(jnp.mean(x * x, axis=-1, keepdims=True) + EPS)
        h = (y * g_ref[...]) * (1.0 + mod_ref[1:2, :]) + mod_ref[0:1, :]
        hb = h.astype(BF16)
        h_scr[...] = hb
        os_ref[...] = jnp.dot(hb, ws_ref[...], preferred_element_type=F32)

    o_ref[...] = jnp.dot(h_scr[...], w_ref[...], preferred_element_type=F32)


def in_projection(x, mod, mod_row, g, w_main, w_small, tm, tn=1280):
    n_tok = x.shape[0]
    return pl.pallas_call(
        _inproj_kernel,
        grid=(n_tok // tm, MAIN_COLS // tn),
        in_specs=[
            pl.BlockSpec((tm, D_MODEL), lambda i, j: (i, 0)),
            pl.BlockSpec((None, 6, D_MODEL), lambda i, j: (mod_row(tm)(i), 0, 0)),
            pl.BlockSpec((1, D_MODEL), lambda i, j: (0, 0)),
            pl.BlockSpec((D_MODEL, tn), lambda i, j: (0, j)),
            pl.BlockSpec((D_MODEL, LANES), lambda i, j: (0, 0)),
        ],
        out_specs=[
            pl.BlockSpec((tm, tn), lambda i, j: (i, j)),
            pl.BlockSpec((tm, LANES), lambda i, j: (i, 0)),
        ],
        out_shape=[jax.ShapeDtypeStruct((n_tok, MAIN_COLS), F32),
                   jax.ShapeDtypeStruct((n_tok, LANES), F32)],
        scratch_shapes=[pltpu.VMEM((tm, D_MODEL), BF16)],
        compiler_params=_params("parallel", "arbitrary"),
        name="in_projection",
    )(x, mod, g, w_main, w_small)


def _rms_head(x, g):
    return x * lax.rsqrt(jnp.mean(x * x, axis=-1, keepdims=True) + EPS) * g


def _rope(x, cos2, sin2):
    return x * cos2 + pltpu.roll(x, HEAD_DIM // 2, 1) * sin2


def _attn_kernel(*refs, has_ctx):
    if has_ctx:
        (q_ref, k_ref, v_ref, qn_ref, kn_ref, cq_ref, sq_ref, ck_ref, sk_ref, xk_ref, xv_ref,
         o_ref, k_scr, v_scr) = refs
    else:
        (q_ref, k_ref, v_ref, qn_ref, kn_ref, o_ref, ko_ref, vo_ref, k_scr, v_scr) = refs

    @pl.when(pl.program_id(2) == 0)
    def _():
        kn = _rms_head(k_ref[...], kn_ref[...])
        v = v_ref[...]
        if has_ctx:
            kn = _rope(kn, ck_ref[...], sk_ref[...])
        else:
            ko_ref[...] = kn
            vo_ref[...] = v
        k_scr[...] = kn.astype(BF16)
        v_scr[...] = v.astype(BF16)

    kb = k_scr[...]
    vb = v_scr[...]
    if has_ctx:
        xk = xk_ref[...].astype(BF16)
        xv = xv_ref[...].astype(BF16)
    scale = HEAD_DIM ** -0.5 * np.log2(np.e)

    def scores(h):
        q = _rms_head(q_ref[:, h * HEAD_DIM:(h + 1) * HEAD_DIM], qn_ref[...])
        if has_ctx:
            q = _rope(q, cq_ref[...], sq_ref[...])
        qb = (q * scale).astype(BF16)
        return _dot_nt(qb, kb), (_dot_nt(qb, xk) if has_ctx else None)

    def softmax(s1, s2):
        m = jnp.max(s1, axis=-1, keepdims=True)
        if has_ctx:
            m = jnp.maximum(m, jnp.max(s2, axis=-1, keepdims=True))
        p1 = jnp.exp2(s1 - m)
        den = jnp.sum(p1, axis=-1, keepdims=True)
        p2 = None
        if has_ctx:
            p2 = jnp.exp2(s2 - m)
            den = den + jnp.sum(p2, axis=-1, keepdims=True)
            p2 = p2.astype(BF16)
        return p1.astype(BF16), p2, den

    def weighted_values(h, p1, p2, den):
        acc = jnp.dot(p1, vb, preferred_element_type=F32)
        if has_ctx:
            acc = acc + jnp.dot(p2, xv, preferred_element_type=F32)
        o_ref[:, h * HEAD_DIM:(h + 1) * HEAD_DIM] = (acc / den).astype(o_ref.dtype)

    s = {0: scores(0)}
    if ATT_GROUP > 1:
        s[1] = scores(1)
    p = {0: softmax(*s[0])}
    for h in range(ATT_GROUP):
        if h + 1 < ATT_GROUP:
            p[h + 1] = softmax(*s[h + 1])
        weighted_values(h, *p[h])
        if h + 2 < ATT_GROUP:
            s[h + 2] = scores(h + 2)


def attention(proj, q_norm, k_norm, n_seq, seq_len, tq, ctx=None):
    n_tok = n_seq * seq_len
    nq = seq_len // tq
    gw = ATT_GROUP * HEAD_DIM
    has_ctx = ctx is not None
    in_specs = [
        pl.BlockSpec((tq, gw), lambda b, g, i: (b * nq + i, OFF_AQ // gw + g)),
        pl.BlockSpec((seq_len, HEAD_DIM), lambda b, g, i: (b, OFF_AK // HEAD_DIM + g)),
        pl.BlockSpec((seq_len, HEAD_DIM), lambda b, g, i: (b, OFF_AV // HEAD_DIM + g)),
        pl.BlockSpec((1, HEAD_DIM), lambda b, g, i: (0, 0)),
        pl.BlockSpec((1, HEAD_DIM), lambda b, g, i: (0, 0)),
    ]
    args = [proj, proj, proj, q_norm.reshape(1, HEAD_DIM), k_norm.reshape(1, HEAD_DIM)]
    out_specs = [pl.BlockSpec((tq, gw), lambda b, g, i: (b * nq + i, g))]
    out_shape = [jax.ShapeDtypeStruct((n_tok, ATT_WIDTH), BF16)]
    if has_ctx:
        cache_k4, cache_v4, layer, cos2, sin2 = ctx
        past = cache_k4.shape[2]
        in_specs += [
            pl.BlockSpec((tq, HEAD_DIM), lambda b, g, i: (i, 0)),
            pl.BlockSpec((tq, HEAD_DIM), lambda b, g, i: (i, 0)),
            pl.BlockSpec((seq_len, HEAD_DIM), lambda b, g, i: (0, 0)),
            pl.BlockSpec((seq_len, HEAD_DIM), lambda b, g, i: (0, 0)),
            pl.BlockSpec((None, None, past, HEAD_DIM), lambda b, g, i: (b, layer, 0, g)),
            pl.BlockSpec((None, None, past, HEAD_DIM), lambda b, g, i: (b, layer, 0, g)),
        ]
        args += [cos2, sin2, cos2, sin2, cache_k4, cache_v4]
    else:
        kv_w = ATT_KV_HEADS * HEAD_DIM
        out_specs += [pl.BlockSpec((seq_len, HEAD_DIM), lambda b, g, i: (b, g)),
                      pl.BlockSpec((seq_len, HEAD_DIM), lambda b, g, i: (b, g))]
        out_shape += [jax.ShapeDtypeStruct((n_tok, kv_w), F32),
                      jax.ShapeDtypeStruct((n_tok, kv_w), F32)]
    return pl.pallas_call(
        functools.partial(_attn_kernel, has_ctx=has_ctx),
        grid=(n_seq, ATT_KV_HEADS, nq),
        in_specs=in_specs,
        out_specs=out_specs,
        out_shape=out_shape,
        scratch_shapes=[pltpu.VMEM((seq_len, HEAD_DIM), BF16), pltpu.VMEM((seq_len, HEAD_DIM), BF16)],
        compiler_params=_params("parallel", "parallel", "arbitrary"),
        name="attention_latent" if has_ctx else "attention_context",
    )(*args)


def _iota2(shape, dim):
    return lax.broadcasted_iota(jnp.int32, shape, dim)


def _conv_chunk(x_ref, w, c, n_chunks, bias=None, cols=slice(None)):
    seq_len = n_chunks * CHUNK
    r0 = pl.multiple_of(c * CHUNK, CHUNK)
    cur = x_ref[pl.ds(r0, CHUNK), cols]
    lo = pl.multiple_of(jnp.maximum(r0 - SUBLANES, 0), SUBLANES)
    hi = pl.multiple_of(jnp.minimum(r0 + CHUNK, seq_len - SUBLANES), SUBLANES)
    prev = jnp.where(c > 0, x_ref[pl.ds(lo, SUBLANES), cols], 0.0)
    nxt = jnp.where(c < n_chunks - 1, x_ref[pl.ds(hi, SUBLANES), cols], 0.0)
    ext = jnp.concatenate([prev, cur, nxt], axis=0)
    n_ext = CHUNK + 2 * SUBLANES
    pad = CONV_TAPS // 2
    acc = None
    for j in range(CONV_TAPS):
        shifted = ext if j == pad else pltpu.roll(ext, (pad - j) % n_ext, 0)
        term = shifted[SUBLANES:SUBLANES + CHUNK, :] * w[j:j + 1, :]
        acc = term if acc is None else acc + term
    if bias is not None:
        acc = acc + bias
    return acc


def _chunk_cumsums(a):
    tril = (_iota2((CHUNK, CHUNK), 1) <= _iota2((CHUNK, CHUNK), 0)).astype(BF16)
    hi, mid, lo = _split3(a)
    pre = (jnp.dot(tril, hi, preferred_element_type=F32)
           + jnp.dot(tril, mid, preferred_element_type=F32)
           + jnp.dot(tril, lo, preferred_element_type=F32))
    tot = pre[CHUNK - 1:CHUNK, :]
    suf = tot - pre + a
    return pre, suf, tot


def _pick_col(x, lane_ids, c):
    return jnp.sum(jnp.where(lane_ids == c, x, 0.0), axis=1, keepdims=True)


def _pick_row(x, row_ids, r):
    return jnp.sum(jnp.where(row_ids == r, x, 0.0), axis=0, keepdims=True)


def _scan_order_cumsum(a_col, a_row, lower):
    ii = _iota2((CHUNK, CHUNK), 0)
    jj = _iota2((CHUNK, CHUNK), 1)
    tril = (jj <= ii).astype(BF16)
    triu = (jj >= ii).astype(BF16)
    left, right = (tril, triu) if lower else (triu, tril)
    g_col = sum(jnp.dot(left, p, preferred_element_type=F32)
                for p in _split3(jnp.broadcast_to(a_col, (CHUNK, CHUNK))))
    g_row = sum(jnp.dot(p, right, preferred_element_type=F32)
                for p in _split3(jnp.broadcast_to(a_row, (CHUNK, CHUNK))))
    tot = g_col[CHUNK - 1:CHUNK, :] if lower else g_col[0:1, :]
    return g_col, g_row, tot


def _decay_matrix(g_col, g_row, incl):
    d = g_col - g_row
    return jnp.where(incl, jnp.exp(jnp.where(incl, d, 0.0)), 0.0)


def _unit_tri_inverses(lms, lowers):
    ii = _iota2((CHUNK, CHUNK), 0)
    jj = _iota2((CHUNK, CHUNK), 1)
    xs = None
    s = 1
    while s < CHUNK:
        sh = s.bit_length() - 1
        bi = ii >> sh
        bj = jj >> sh
        same = (bi >> 1) == (bj >> 1)
        m_lower = same & ((bi & 1) == 1) & ((bj & 1) == 0)
        m_upper = same & ((bi & 1) == 0) & ((bj & 1) == 1)
        offs = [jnp.where(m_lower if lower else m_upper, lm, 0.0) for lm, lower in zip(lms, lowers)]
        if xs is None:
            eye = (ii == jj).astype(F32)
            xs = [eye - off for off in offs]
        else:
            ps = [_dot(off, x) for off, x in zip(offs, xs)]
            xs = [x - _dot(x, p) for x, p in zip(xs, ps)]
        s *= 2
    return xs


def _gdn_kernel(*refs, n_chunks, has_state, hps):
    (q_ref, k_ref, v_ref, gate_ref, sm_ref, wq_ref, wk_ref, wv_ref, alog_ref, dtb_ref, gn_ref) = refs[:11]
    pos = 11
    if has_state:
        s0_ref = refs[pos]
        pos += 1
    out_ref, sfin_ref = refs[pos:pos + 2]
    (u_scr, w_scr, qk_scr, qg_scr, kg_scr, egl_scr, o_scr) = refs[pos + 2:]
    h0 = pl.program_id(1) * hps
    lane_ids = _iota2((CHUNK, LANES), 1)
    ii = _iota2((CHUNK, CHUNK), 0)
    jj = _iota2((CHUNK, CHUNK), 1)
    neg_a = -jnp.exp(alog_ref[...])
    dtb = dtb_ref[...]
    prep_chunks = int(np.gcd(n_chunks, max(GDN_PREP_CHUNKS // hps, 1)))
    problems = [(d, hh) for d in range(N_DIR) for hh in range(hps)]

    def prep(i, carry):
        chains = []
        for cc in range(prep_chunks):
            c = i * prep_chunks + cc
            r0 = pl.multiple_of(c * CHUNK, CHUNK)
            sm = sm_ref[pl.ds(r0, CHUNK), :]
            beta_all = jax.nn.sigmoid(sm)
            la_all = neg_a * _softplus(sm + dtb)
            la_t = la_all.T
            for hh in range(hps):
                cols = slice(hh * LANES, (hh + 1) * LANES)
                q = _silu(_conv_chunk(q_ref, wq_ref[:, cols], c, n_chunks, cols=cols))
                k = _silu(_conv_chunk(k_ref, wk_ref[:, cols], c, n_chunks, cols=cols))
                v = _silu(_conv_chunk(v_ref, wv_ref[:, cols], c, n_chunks, cols=cols))
                q = q * lax.rsqrt(jnp.sum(q * q, axis=-1, keepdims=True) + EPS) * (GDN_DK ** -0.5)
                k = k * lax.rsqrt(jnp.sum(k * k, axis=-1, keepdims=True) + EPS)
                kk = _dot_nt(k, k)
                qk = _dot_nt(q, k)
                for d in range(N_DIR):
                    lower = d == 0
                    col = SM_DECAY + d * GDN_HEADS + h0 + hh
                    g_col, g_row, g_last = _scan_order_cumsum(
                        _pick_col(la_all, lane_ids, col), _pick_row(la_t, ii, col), lower)
                    beta = _pick_col(beta_all, lane_ids, SM_BETA + d * GDN_HEADS + h0 + hh)
                    incl = (jj <= ii) if lower else (jj >= ii)
                    strict = (jj < ii) if lower else (jj > ii)
                    decay = _decay_matrix(g_col, g_row, incl)
                    chains.append(dict(c=c, r0=r0, d=d, p=d * hps + hh, q=q, k=k, v=v, qk=qk, beta=beta,
                                       decay=decay, g_col=g_col, g_last=g_last,
                                       lm=jnp.where(strict, kk * beta * decay, 0.0)))
        ts = _unit_tri_inverses([ch['lm'] for ch in chains], [ch['d'] == 0 for ch in chains])
        egs = [jnp.exp(ch['g_col']) for ch in chains]
        us = [_dot(t, ch['v'] * ch['beta']) for t, ch in zip(ts, chains)]
        ws = [_dot(t, ch['k'] * (ch['beta'] * eg)) for t, ch, eg in zip(ts, chains, egs)]
        for ch, eg, u, w in zip(chains, egs, us, ws):
            p, rows = ch['p'], pl.ds(ch['r0'], CHUNK)
            u_scr[p, rows, :] = u
            w_scr[p, rows, :] = w.astype(BF16)
            qk_scr[p, rows, :] = (ch['qk'] * ch['decay']).astype(BF16)
            qg_scr[p, rows, :] = (ch['q'] * eg).astype(BF16)
            kg_scr[p, rows, :] = (ch['k'] * jnp.exp(ch['g_last'] - ch['g_col'])).astype(BF16)
            egl_scr[p, pl.ds(ch['c'], 1), :] = jnp.exp(ch['g_last'])
        return carry

    lax.fori_loop(0, n_chunks // prep_chunks, prep, 0)
    o_scr[...] = jnp.zeros_like(o_scr)

    def scan(i, carry):
        cs = (i, n_chunks - 1 - i)
        rows = [pl.ds(pl.multiple_of(c * CHUNK, CHUNK), CHUNK) for c in cs]
        n_p = len(problems)
        sbs = [s.astype(BF16) for s in carry]
        ws = [jnp.dot(w_scr[p, rows[d], :], sbs[p], preferred_element_type=F32)
              for p, (d, _) in enumerate(problems)]
        gs = [jnp.dot(qg_scr[p, rows[d], :], sbs[p], preferred_element_type=F32)
              for p, (d, _) in enumerate(problems)]
        vbs = [(u_scr[p, rows[d], :] - ws[p]).astype(BF16) for p, (d, _) in enumerate(problems)]
        os = [gs[p] + jnp.dot(qk_scr[p, rows[d], :], vbs[p], preferred_element_type=F32)
              for p, (d, _) in enumerate(problems)]
        new = [carry[p] * egl_scr[p, pl.ds(cs[d], 1), :] + _dot_tn(kg_scr[p, rows[d], :], vbs[p])
               for p, (d, _) in enumerate(problems)]
        for p, (d, hh) in enumerate(problems):
            o_scr[hh, rows[d], :] += os[p]
        return tuple(new[:n_p])

    if has_state:
        init = tuple(s0_ref[d, hh] for d, hh in problems)
    else:
        init = tuple(jnp.zeros((GDN_DK, GDN_DV), F32) for _ in problems)
    final = lax.fori_loop(0, n_chunks, scan, init)
    for (d, hh), s_fin in zip(problems, final):
        sfin_ref[d, hh] = s_fin

    def finish(c, carry):
        rows = pl.ds(pl.multiple_of(c * CHUNK, CHUNK), CHUNK)
        for hh in range(hps):
            cols = slice(hh * LANES, (hh + 1) * LANES)
            o = o_scr[hh, rows, :]
            y = o * lax.rsqrt(jnp.mean(o * o, axis=-1, keepdims=True) + EPS) * gn_ref[...]
            out_ref[rows, cols] = (y * _silu(gate_ref[rows, cols])).astype(out_ref.dtype)
        return carry

    lax.fori_loop(0, n_chunks, finish, 0)


def gated_deltanet(proj, small, conv_w, alog_vec, dtb_vec, gnorm, n_seq, seq_len, state=None, hps=1):
    n_tok = n_seq * seq_len
    n_chunks = seq_len // CHUNK
    has_state = state is not None
    nb = GDN_HEADS // hps
    bw = hps * LANES
    in_specs = [
        pl.BlockSpec((seq_len, bw), lambda b, h: (b, h)),
        pl.BlockSpec((seq_len, bw), lambda b, h: (b, nb + h)),
        pl.BlockSpec((seq_len, bw), lambda b, h: (b, 2 * nb + h)),
        pl.BlockSpec((seq_len, bw), lambda b, h: (b, 3 * nb + h)),
        pl.BlockSpec((seq_len, LANES), lambda b, h: (b, 0)),
        pl.BlockSpec((CONV_TAPS, bw), lambda b, h: (0, h)),
        pl.BlockSpec((CONV_TAPS, bw), lambda b, h: (0, nb + h)),
        pl.BlockSpec((CONV_TAPS, bw), lambda b, h: (0, 2 * nb + h)),
        pl.BlockSpec((1, LANES), lambda b, h: (0, 0)),
        pl.BlockSpec((1, LANES), lambda b, h: (0, 0)),
        pl.BlockSpec((1, LANES), lambda b, h: (0, 0)),
    ]
    args = [proj, proj, proj, proj, small, conv_w, conv_w, conv_w, alog_vec, dtb_vec,
            gnorm.reshape(1, GDN_DV)]
    if has_state:
        state_gdn, layer = state
        in_specs.append(pl.BlockSpec((None, None, N_DIR, hps, GDN_DK, GDN_DV),
                                     lambda b, h: (b, layer, 0, h, 0, 0)))
        args.append(state_gdn)
    n_prob = N_DIR * hps
    return pl.pallas_call(
        functools.partial(_gdn_kernel, n_chunks=n_chunks, has_state=has_state, hps=hps),
        grid=(n_seq, nb),
        in_specs=in_specs,
        out_specs=[
            pl.BlockSpec((seq_len, bw), lambda b, h: (b, h)),
            pl.BlockSpec((None, N_DIR, hps, GDN_DK, GDN_DV), lambda b, h: (b, 0, h, 0, 0)),
        ],
        out_shape=[jax.ShapeDtypeStruct((n_tok, GDN_WIDTH), BF16),
                   jax.ShapeDtypeStruct((n_seq, N_DIR, GDN_HEADS, GDN_DK, GDN_DV), F32)],
        scratch_shapes=[
            pltpu.VMEM((n_prob, seq_len, GDN_DV), F32),
            pltpu.VMEM((n_prob, seq_len, GDN_DK), BF16),
            pltpu.VMEM((n_prob, seq_len, CHUNK), BF16),
            pltpu.VMEM((n_prob, seq_len, GDN_DK), BF16),
            pltpu.VMEM((n_prob, seq_len, GDN_DK), BF16),
            pltpu.VMEM((n_prob, max(n_chunks, SUBLANES), LANES), F32),
            pltpu.VMEM((hps, seq_len, GDN_DV), F32),
        ],
        compiler_params=_params("parallel", "parallel"),
        name="gdn_latent" if has_state else "gdn_context",
    )(*args)


def _ssd_kernel(*refs, n_chunks, has_state):
    (x_ref, b_ref, c_ref, z_ref, sm_ref, wx_ref, wb_ref, wc_ref, bx_ref, bb_ref, bc_ref,
     alog_ref, dtb_ref, dvec_ref, gn_ref) = refs[:15]
    pos = 15
    if has_state:
        h0_ref = refs[pos]
        pos += 1
    out_ref, hfin_ref = refs[pos:pos + 2]
    xs_scr, bs_scr, cs_scr, y_scr, h_scr = refs[pos + 2:]
    g = pl.program_id(1)
    hpg = SSD_HEADS_PER_GROUP
    P = SSD_HEAD_DIM
    lane_ids = _iota2((CHUNK, LANES), 1)
    lane_row = _iota2((1, LANES), 1)
    ii = _iota2((CHUNK, CHUNK), 0)
    jj = _iota2((CHUNK, CHUNK), 1)
    neg_a = -jnp.exp(alog_ref[...])
    dtb = dtb_ref[...]

    def prep(c, carry):
        r0 = pl.multiple_of(c * CHUNK, CHUNK)
        xs_scr[pl.ds(r0, CHUNK), :] = _silu(_conv_chunk(x_ref, wx_ref[...], c, n_chunks, bx_ref[...]))
        bs_scr[pl.ds(r0, CHUNK), :] = _silu(_conv_chunk(b_ref, wb_ref[...], c, n_chunks, bb_ref[...]))
        cs_scr[pl.ds(r0, CHUNK), :] = _silu(_conv_chunk(c_ref, wc_ref[...], c, n_chunks, bc_ref[...]))
        return carry

    lax.fori_loop(0, n_chunks, prep, 0)
    y_scr[...] = jnp.zeros_like(y_scr)
    if has_state:
        h_scr[...] = h0_ref[...]
    else:
        h_scr[...] = jnp.zeros_like(h_scr)

    def scan(i, carry):
        work = []
        for d, c in ((0, i), (1, n_chunks - 1 - i)):
            rows = pl.ds(pl.multiple_of(c * CHUNK, CHUNK), CHUNK)
            lower = d == 0
            x = xs_scr[rows, :]
            bm = bs_scr[rows, :]
            cm = cs_scr[rows, :]
            dt_all = _softplus(sm_ref[rows, :] + dtb)
            pre, suf, tot = _chunk_cumsums(dt_all * neg_a)
            ac = pre if lower else suf
            ac_t = ac.T
            cb = _dot_nt(cm, bm)
            incl = (jj <= ii) if lower else (jj >= ii)
            for hh in range(hpg):
                col = SM_DT + d * SSD_HEADS + g * hpg + hh
                a_col = _pick_col(ac, lane_ids, col)
                a_last = jnp.sum(jnp.where(lane_row == col, tot, 0.0), axis=1, keepdims=True)
                work.append(dict(
                    d=d, hh=hh, rows=rows,
                    xdt=x[:, hh * P:(hh + 1) * P] * _pick_col(dt_all, lane_ids, col),
                    cbd=cb * _decay_matrix(a_col, _pick_row(ac_t, ii, col), incl),
                    b_dec=bm * jnp.exp(a_last - a_col),
                    c_dec=cm * jnp.exp(a_col),
                    e_last=jnp.exp(a_last)))
        y_in = [_dot(w['cbd'], w['xdt']) for w in work]
        sts = [_dot_tn(w['xdt'], w['b_dec']) for w in work]
        y_off = [_dot_nt(w['c_dec'], h_scr[w['d'], w['hh']]) for w in work]
        for w, yi, st, yo in zip(work, y_in, sts, y_off):
            d, hh = w['d'], w['hh']
            h_scr[d, hh] = h_scr[d, hh] * w['e_last'] + st
            y_scr[w['rows'], hh * P:(hh + 1) * P] += yi + yo
        return carry

    lax.fori_loop(0, n_chunks, scan, 0)
    hfin_ref[...] = h_scr[...]

    def finish(c, carry):
        r0 = pl.multiple_of(c * CHUNK, CHUNK)
        y = y_scr[pl.ds(r0, CHUNK), :] + dvec_ref[...] * xs_scr[pl.ds(r0, CHUNK), :]
        y = y * _silu(z_ref[pl.ds(r0, CHUNK), :])
        y = y * lax.rsqrt(jnp.mean(y * y, axis=-1, keepdims=True) + EPS) * gn_ref[...]
        out_ref[pl.ds(r0, CHUNK), :] = y.astype(out_ref.dtype)
        return carry

    lax.fori_loop(0, n_chunks, finish, 0)


def ssd_mixer(proj, small, conv_w, conv_b, alog_vec, dtb_vec, d_vec, gnorm, n_seq, seq_len, state=None):
    n_tok = n_seq * seq_len
    n_chunks = seq_len // CHUNK
    has_state = state is not None
    gw = SSD_GROUP_WIDTH
    hpg = SSD_HEADS_PER_GROUP
    xb = OFF_SXBC // gw
    bb = (OFF_SXBC + SSD_WIDTH) // LANES
    cb = bb + SSD_GROUPS
    zb = OFF_SZ // gw
    in_specs = [
        pl.BlockSpec((seq_len, gw), lambda b, g: (b, xb + g)),
        pl.BlockSpec((seq_len, LANES), lambda b, g: (b, bb + g)),
        pl.BlockSpec((seq_len, LANES), lambda b, g: (b, cb + g)),
        pl.BlockSpec((seq_len, gw), lambda b, g: (b, zb + g)),
        pl.BlockSpec((seq_len, LANES), lambda b, g: (b, 0)),
        pl.BlockSpec((CONV_TAPS, gw), lambda b, g: (0, g)),
        pl.BlockSpec((CONV_TAPS, LANES), lambda b, g: (0, SSD_WIDTH // LANES + g)),
        pl.BlockSpec((CONV_TAPS, LANES), lambda b, g: (0, SSD_WIDTH // LANES + SSD_GROUPS + g)),
        pl.BlockSpec((1, gw), lambda b, g: (0, g)),
        pl.BlockSpec((1, LANES), lambda b, g: (0, SSD_WIDTH // LANES + g)),
        pl.BlockSpec((1, LANES), lambda b, g: (0, SSD_WIDTH // LANES + SSD_GROUPS + g)),
        pl.BlockSpec((1, LANES), lambda b, g: (0, 0)),
        pl.BlockSpec((1, LANES), lambda b, g: (0, 0)),
        pl.BlockSpec((1, gw), lambda b, g: (0, g)),
        pl.BlockSpec((1, gw), lambda b, g: (0, g)),
    ]
    conv_b2 = conv_b.reshape(1, -1)
    args = [proj, proj, proj, proj, small, conv_w, conv_w, conv_w, conv_b2, conv_b2, conv_b2,
            alog_vec, dtb_vec, d_vec, gnorm.reshape(1, SSD_WIDTH)]
    if has_state:
        state_ssd, layer = state
        in_specs.append(pl.BlockSpec((None, None, N_DIR, hpg, SSD_HEAD_DIM, SSD_STATE),
                                     lambda b, g: (b, layer, 0, g, 0, 0)))
        args.append(state_ssd)
    return pl.pallas_call(
        functools.partial(_ssd_kernel, n_chunks=n_chunks, has_state=has_state),
        grid=(n_seq, SSD_GROUPS),
        in_specs=in_specs,
        out_specs=[
            pl.BlockSpec((seq_len, gw), lambda b, g: (b, g)),
            pl.BlockSpec((None, N_DIR, hpg, SSD_HEAD_DIM, SSD_STATE), lambda b, g: (b, 0, g, 0, 0)),
        ],
        out_shape=[jax.ShapeDtypeStruct((n_tok, SSD_WIDTH), BF16),
                   jax.ShapeDtypeStruct((n_seq, N_DIR, SSD_HEADS, SSD_HEAD_DIM, SSD_STATE), F32)],
        scratch_shapes=[
            pltpu.VMEM((seq_len, gw), F32),
            pltpu.VMEM((seq_len, SSD_STATE), F32),
            pltpu.VMEM((seq_len, SSD_STATE), F32),
            pltpu.VMEM((seq_len, gw), F32),
            pltpu.VMEM((N_DIR, hpg, SSD_HEAD_DIM, SSD_STATE), F32),
        ],
        compiler_params=_params("parallel", "parallel"),
        name="ssd_latent" if has_state else "ssd_context",
    )(*args)


def _top2_sum(a, b, c, d):
    hi1, lo1 = jnp.maximum(a, b), jnp.minimum(a, b)
    hi2, lo2 = jnp.maximum(c, d), jnp.minimum(c, d)
    return jnp.maximum(hi1, hi2) + jnp.maximum(jnp.minimum(hi1, hi2), jnp.maximum(lo1, lo2))


def _outproj_kernel(x_ref, gdn_ref, att_ref, ssd_ref, mod_ref, g_ref, w_ref, wr_ref, br_ref,
                    xo_ref, h_ref, route_ref):
    mixed = (jnp.dot(gdn_ref[...], w_ref[0:GDN_WIDTH, :], preferred_element_type=F32)
             + jnp.dot(att_ref[...], w_ref[GDN_WIDTH:GDN_WIDTH + ATT_WIDTH, :], preferred_element_type=F32)
             + jnp.dot(ssd_ref[...], w_ref[GDN_WIDTH + ATT_WIDTH:, :], preferred_element_type=F32))
    x = x_ref[...] + mod_ref[2:3, :] * mixed
    xo_ref[...] = x
    y = x * lax.rsqrt(jnp.mean(x * x, axis=-1, keepdims=True) + EPS)
    h = (y * g_ref[...]) * (1.0 + mod_ref[4:5, :]) + mod_ref[3:4, :]
    h_ref[...] = h
    lt = _dot_nt(wr_ref[...], h)
    sc = [jax.nn.sigmoid(lt[e:e + 1, :]) for e in range(N_EXPERTS)]
    sel = [sc[e] + br_ref[e:e + 1, :] for e in range(N_EXPERTS)]
    epg = EXPERTS_PER_GROUP
    gs = [_top2_sum(*sel[gi * epg:(gi + 1) * epg]) for gi in range(N_EXPERT_GROUPS)]
    best, bg = gs[0], jnp.zeros_like(gs[0], dtype=jnp.int32)
    for gi in range(1, N_EXPERT_GROUPS):
        upd = gs[gi] > best
        bg = jnp.where(upd, gi, bg)
        best = jnp.where(upd, gs[gi], best)
    cand_sel, cand_sc = [], []
    for k in range(epg):
        vs, vc = sel[k], sc[k]
        for gi in range(1, N_EXPERT_GROUPS):
            vs = jnp.where(bg == gi, sel[gi * epg + k], vs)
            vc = jnp.where(bg == gi, sc[gi * epg + k], vc)
        cand_sel.append(vs)
        cand_sc.append(vc)
    m1, i1, g1 = cand_sel[0], jnp.zeros_like(bg), cand_sc[0]
    for k in range(1, epg):
        upd = cand_sel[k] > m1
        i1 = jnp.where(upd, k, i1)
        g1 = jnp.where(upd, cand_sc[k], g1)
        m1 = jnp.where(upd, cand_sel[k], m1)
    m2 = jnp.full_like(m1, -jnp.inf)
    i2, g2 = jnp.full_like(bg, -1), jnp.zeros_like(g1)
    for k in range(epg):
        upd = (i1 != k) & ((cand_sel[k] > m2) | (i2 < 0))
        i2 = jnp.where(upd, k, i2)
        g2 = jnp.where(upd, cand_sc[k], g2)
        m2 = jnp.where(upd, cand_sel[k], m2)
    den = g1 + g2
    route_ref[...] = jnp.zeros_like(route_ref)
    route_ref[0:1, :] = (bg * epg + i1).astype(F32)
    route_ref[1:2, :] = (bg * epg + i2).astype(F32)
    route_ref[2:3, :] = g1 / den
    route_ref[3:4, :] = g2 / den


def out_projection(x, gdn, att, ssd, mod, mod_row, g, w_out, w_router, b_router, tm=512):
    n_tok = x.shape[0]
    return pl.pallas_call(
        _outproj_kernel,
        grid=(n_tok // tm,),
        in_specs=[
            pl.BlockSpec((tm, D_MODEL), lambda i: (i, 0)),
            pl.BlockSpec((tm, GDN_WIDTH), lambda i: (i, 0)),
            pl.BlockSpec((tm, ATT_WIDTH), lambda i: (i, 0)),
            pl.BlockSpec((tm, SSD_WIDTH), lambda i: (i, 0)),
            pl.BlockSpec((None, 6, D_MODEL), lambda i: (mod_row(tm)(i), 0, 0)),
            pl.BlockSpec((1, D_MODEL), lambda i: (0, 0)),
            pl.BlockSpec((MIX_WIDTH, D_MODEL), lambda i: (0, 0)),
            pl.BlockSpec((LANES, D_MODEL), lambda i: (0, 0)),
            pl.BlockSpec((LANES, 1), lambda i: (0, 0)),
        ],
        out_specs=[
            pl.BlockSpec((tm, D_MODEL), lambda i: (i, 0)),
            pl.BlockSpec((tm, D_MODEL), lambda i: (i, 0)),
            pl.BlockSpec((SUBLANES, tm), lambda i: (0, i)),
        ],
        out_shape=[jax.ShapeDtypeStruct((n_tok, D_MODEL), F32),
                   jax.ShapeDtypeStruct((n_tok, D_MODEL), F32),
                   jax.ShapeDtypeStruct((SUBLANES, n_tok), F32)],
        compiler_params=_params("parallel"),
        name="out_projection",
    )(x, gdn, att, ssd, mod, g, w_out, w_router, b_router)


def _expert_kernel(idx_ref, be_ref, nb_ref, h_ref, wg_ref, wu_ref, wd_ref, o_ref, xbuf, sem, *, tm):
    i = pl.program_id(0)
    n_used = nb_ref[0]

    def row_copy(step, slot, r):
        return pltpu.make_async_copy(h_ref.at[pl.ds(idx_ref[step * tm + r], 1)],
                                     xbuf.at[slot, pl.ds(r, 1)], sem.at[slot])

    def start_rows(step, slot):
        def body(r2, carry):
            row_copy(step, slot, 2 * r2).start(priority=0)
            row_copy(step, slot, 2 * r2 + 1).start(priority=1)
            return carry
        lax.fori_loop(0, tm // 2, body, 0, unroll=4)

    def wait_rows(step, slot):
        def body(r, carry):
            row_copy(step, slot, r).wait()
            return carry
        lax.fori_loop(0, tm, body, 0, unroll=8)

    @pl.when(i == 0)
    def _():
        start_rows(0, 0)

    @pl.when(i < n_used)
    def _():
        slot = i % 2
        wait_rows(i, slot)
        nxt = jnp.minimum(i + 1, n_used - 1)
        for r in range(tm):
            row_copy(nxt, 1 - slot, r).start(priority=r % 2)
        xb = xbuf[slot].astype(BF16)
        a = jnp.dot(xb, wg_ref[...], preferred_element_type=F32)
        u = jnp.dot(xb, wu_ref[...], preferred_element_type=F32)
        o_ref[...] = jnp.dot((_silu(a) * u).astype(BF16), wd_ref[...], preferred_element_type=F32)

        @pl.when(i + 1 >= n_used)
        def _():
            wait_rows(nxt, 1 - slot)

    @pl.when(i >= n_used)
    def _():
        o_ref[...] = jnp.zeros_like(o_ref)


def _cast_kernel(x_ref, o_ref):
    o_ref[...] = x_ref[...].astype(o_ref.dtype)


def cast_bf16(x):
    n, r, c = x.shape
    return pl.pallas_call(
        _cast_kernel,
        grid=(n,),
        in_specs=[pl.BlockSpec((None, r, c), lambda i: (i, 0, 0))],
        out_specs=pl.BlockSpec((None, r, c), lambda i: (i, 0, 0)),
        out_shape=jax.ShapeDtypeStruct((n, r, c), BF16),
        compiler_params=_params("parallel"),
        name="cast_bf16",
    )(x)


def expert_mlp(row_tok, block_e, n_used, h, w_gate, w_up, w_down, tm, layer):
    n_rows = row_tok.shape[0]
    e0 = layer * N_EXPERTS
    return pl.pallas_call(
        functools.partial(_expert_kernel, tm=tm),
        grid_spec=pltpu.PrefetchScalarGridSpec(
            num_scalar_prefetch=3,
            grid=(n_rows // tm,),
            in_specs=[
                pl.BlockSpec(memory_space=pl.ANY),
                pl.BlockSpec((None, D_MODEL, EXPERT_FF), lambda i, idx, be, nb: (e0 + be[i], 0, 0)),
                pl.BlockSpec((None, D_MODEL, EXPERT_FF), lambda i, idx, be, nb: (e0 + be[i], 0, 0)),
                pl.BlockSpec((None, EXPERT_FF, D_MODEL), lambda i, idx, be, nb: (e0 + be[i], 0, 0)),
            ],
            out_specs=pl.BlockSpec((tm, D_MODEL), lambda i, idx, be, nb: (i, 0)),
            scratch_shapes=[pltpu.VMEM((2, tm, D_MODEL), F32), pltpu.SemaphoreType.DMA((2,))],
        ),
        out_shape=jax.ShapeDtypeStruct((n_rows, D_MODEL), F32),
        compiler_params=_params("arbitrary"),
        name="expert_mlp",
    )(row_tok, block_e, n_used, h, w_gate, w_up, w_down)


def _combine_kernel(dest_ref, x_ref, gates_ref, mod_ref, y_ref, o_ref, buf, sem, *, tm, n_tok):
    i = pl.program_id(0)

    def copy(step, slot, k, r):
        return pltpu.make_async_copy(y_ref.at[pl.ds(dest_ref[k * n_tok + step * tm + r], 1)],
                                     buf.at[slot, k, pl.ds(r, 1)], sem.at[slot])

    def start_rows(step, slot):
        def body(r, carry):
            for k in range(TOP_K):
                copy(step, slot, k, r).start(priority=k)
            return carry
        lax.fori_loop(0, tm, body, 0, unroll=4)

    def wait_rows(step, slot):
        def body(r, carry):
            for k in range(TOP_K):
                copy(step, slot, k, r).wait()
            return carry
        lax.fori_loop(0, tm, body, 0, unroll=4)

    @pl.when(i == 0)
    def _():
        start_rows(0, 0)

    @pl.when(i + 1 < pl.num_programs(0))
    def _():
        start_rows(i + 1, (i + 1) % 2)

    slot = i % 2
    wait_rows(i, slot)
    moe = gates_ref[:, 0:1] * buf[slot, 0] + gates_ref[:, 1:2] * buf[slot, 1]
    o_ref[...] = x_ref[...] + mod_ref[5:6, :] * moe


def moe_combine(x, y_rows, dest, gates, mod, mod_row, tm=256):
    n_tok = x.shape[0]
    return pl.pallas_call(
        functools.partial(_combine_kernel, tm=tm, n_tok=n_tok),
        grid_spec=pltpu.PrefetchScalarGridSpec(
            num_scalar_prefetch=1,
            grid=(n_tok // tm,),
            in_specs=[
                pl.BlockSpec((tm, D_MODEL), lambda i, d: (i, 0)),
                pl.BlockSpec((tm, TOP_K), lambda i, d: (i, 0)),
                pl.BlockSpec((None, 6, D_MODEL), lambda i, d: (mod_row(tm)(i), 0, 0)),
                pl.BlockSpec(memory_space=pl.ANY),
            ],
            out_specs=pl.BlockSpec((tm, D_MODEL), lambda i, d: (i, 0)),
            scratch_shapes=[pltpu.VMEM((2, TOP_K, tm, D_MODEL), F32), pltpu.SemaphoreType.DMA((2,))],
        ),
        out_shape=jax.ShapeDtypeStruct((n_tok, D_MODEL), F32),
        compiler_params=_params("arbitrary"),
        name="moe_combine",
    )(dest, x, gates, mod, y_rows)


def dispatch_plan(route, tm):
    n_tok = route.shape[1]
    e_idx = route[0:TOP_K].astype(jnp.int32)
    flat_e = e_idx.reshape(-1)
    n_assign = flat_e.shape[0]
    onehot = (flat_e[:, None] == jnp.arange(N_EXPERTS, dtype=jnp.int32)[None, :]).astype(jnp.int32)
    csum = jnp.cumsum(onehot, axis=0)
    rank = jnp.sum(onehot * csum, axis=1) - 1
    counts = csum[-1]
    padded = (counts + tm - 1) // tm * tm
    padded_end = jnp.cumsum(padded)
    padded_start = padded_end - padded
    dest = padded_start[flat_e] + rank
    n_blocks = n_assign // tm + N_EXPERTS
    n_rows = n_blocks * tm
    tok = jnp.tile(jnp.arange(n_tok, dtype=jnp.int32), TOP_K)
    row_tok = jnp.zeros((n_rows,), jnp.int32).at[dest].set(tok)
    block_e = jnp.minimum(
        jnp.searchsorted(padded_end, jnp.arange(n_blocks, dtype=jnp.int32) * tm, side='right'),
        N_EXPERTS - 1).astype(jnp.int32)
    gates = route[TOP_K:2 * TOP_K].T
    n_used = (padded_end[-1:] // tm).astype(jnp.int32)
    return row_tok, block_e, n_used, dest, gates


def _lane_vec(pairs):
    v = jnp.zeros((LANES,), F32)
    for off, arr in pairs:
        v = lax.dynamic_update_slice(v, arr.reshape(-1).astype(F32), (off,))
    return v.reshape(1, LANES)


def _rope_tables(n_tok):
    n_rows = n_tok // GRID_W
    rows = np.repeat(np.arange(n_rows), GRID_W).astype(np.float32)
    cols = np.tile(np.arange(GRID_W), n_rows).astype(np.float32)
    inv = 1.0 / (ROPE_THETA ** (jnp.arange(ROPE_AXIS_PAIRS, dtype=F32) / ROPE_AXIS_PAIRS))
    ang = jnp.concatenate([rows[:, None] * inv, cols[:, None] * inv], axis=-1)
    cos, sin = jnp.cos(ang), jnp.sin(ang)
    return jnp.concatenate([cos, cos], axis=-1), jnp.concatenate([-sin, sin], axis=-1)


def _reorder_w_in(w):
    c0 = 4 * GDN_WIDTH
    c1 = c0 + 2 * N_DIR * GDN_HEADS
    c2 = w.shape[1] - N_DIR * SSD_HEADS
    main = jnp.concatenate([w[:, :c0], w[:, c1:c2]], axis=1)
    small = jnp.concatenate([w[:, c0:c1], w[:, c2:],
                             jnp.zeros((w.shape[0], LANES - SM_USED), w.dtype)], axis=1)
    return main.astype(BF16), small.astype(BF16)


def _group_layer(x, mod, mod_row, lp, shared, n_seq, seq_len, tm_in, tq, moe_tm, ctx):
    proj, small = in_projection(x, mod, mod_row, lp['norm_mix'], lp['w_in_main'], lp['w_in_small'], tm_in)
    if ctx is None:
        att, k_new, v_new = attention(proj, lp['q_norm'], lp['k_norm'], n_seq, seq_len, tq)
        gdn, s_gdn = gated_deltanet(proj, small, lp['gdn_conv'], lp['gdn_alog_vec'], lp['gdn_dtb_vec'],
                                    lp['gdn_norm'], n_seq, seq_len, hps=GDN_HEADS)
        ssd, s_ssd = ssd_mixer(proj, small, lp['ssd_conv_w'], lp['ssd_conv_b'], lp['ssd_alog_vec'],
                               lp['ssd_dtb_vec'], lp['ssd_d_vec'], lp['ssd_norm'], n_seq, seq_len)
        new_ctx = (k_new, v_new, s_gdn, s_ssd)
    else:
        cache_k4, cache_v4, state_gdn, state_ssd, layer, cos2, sin2 = ctx
        att = attention(proj, lp['q_norm'], lp['k_norm'], n_seq, seq_len, tq,
                        ctx=(cache_k4, cache_v4, layer, cos2, sin2))[0]
        gdn, _ = gated_deltanet(proj, small, lp['gdn_conv'], lp['gdn_alog_vec'], lp['gdn_dtb_vec'],
                                lp['gdn_norm'], n_seq, seq_len, state=(state_gdn, layer))
        ssd, _ = ssd_mixer(proj, small, lp['ssd_conv_w'], lp['ssd_conv_b'], lp['ssd_alog_vec'],
                           lp['ssd_dtb_vec'], lp['ssd_d_vec'], lp['ssd_norm'], n_seq, seq_len,
                           state=(state_ssd, layer))
        new_ctx = None
    x_mid, h2, route = out_projection(x, gdn, att, ssd, mod, mod_row, lp['norm_ffn'], lp['w_out'],
                                      shared['w_router'], shared['b_router'])
    row_tok, block_e, n_used, dest, gates = dispatch_plan(route, moe_tm)
    y_rows = expert_mlp(row_tok, block_e, n_used, h2, shared['w_gate'], shared['w_up'], shared['w_down'],
                        moe_tm, lp['layer'])
    x_out = moe_combine(x_mid, y_rows, dest, gates, mod, mod_row)
    return x_out, new_ctx


def kernel(x_prompt, x_sample, c, cache_k, cache_v, state_gdn, state_ssd, c_ctx, w_ada, b_ada, norm_mix,
           norm_ffn, w_in, gdn_conv, gdn_a_log, gdn_dt_bias, gdn_norm, q_norm, k_norm, ssd_conv_w,
           ssd_conv_b, ssd_a_log, ssd_dt_bias, ssd_d, ssd_norm, w_out, w_router, b_router, w_gate, w_up,
           w_down):
    n_ctx, ctx_len, _ = x_prompt.shape
    n_lat, lat_len, _ = x_sample.shape
    past = cache_k.shape[2]
    kv_w = ATT_KV_HEADS * HEAD_DIM

    cond = jnp.concatenate([c_ctx[None, :], c, jnp.zeros((SUBLANES - 1 - n_lat, D_MODEL), F32)], axis=0)
    mod_all = ada_modulation(cond, w_ada, b_ada).reshape(DEPTH, SUBLANES, 6, D_MODEL)

    shared = {
        'w_router': jnp.concatenate([w_router.T, jnp.zeros((LANES - N_EXPERTS, D_MODEL), F32)],
                                    axis=0).astype(BF16),
        'b_router': jnp.concatenate([b_router, jnp.zeros((LANES - N_EXPERTS,), F32)]).reshape(LANES, 1),
    }
    cos2, sin2 = _rope_tables(lat_len)
    cache_k4 = cache_k.reshape(n_lat, DEPTH, past, kv_w)
    cache_v4 = cache_v.reshape(n_lat, DEPTH, past, kv_w)

    shared['w_gate'] = cast_bf16(w_gate.reshape(DEPTH * N_EXPERTS, D_MODEL, EXPERT_FF))
    shared['w_up'] = cast_bf16(w_up.reshape(DEPTH * N_EXPERTS, D_MODEL, EXPERT_FF))
    shared['w_down'] = cast_bf16(w_down.reshape(DEPTH * N_EXPERTS, EXPERT_FF, D_MODEL))
    w_out_b = cast_bf16(w_out)
    tm_ctx = 1024
    tm_lat = 1024

    y_ctx = x_prompt.reshape(n_ctx * ctx_len, D_MODEL)
    y_lat = x_sample.reshape(n_lat * lat_len, D_MODEL)
    new_k, new_v, new_gdn, new_ssd = [], [], [], []
    for l in range(DEPTH):
        w_main, w_small = _reorder_w_in(w_in[l])
        lp = {
            'norm_mix': norm_mix[l].reshape(1, D_MODEL), 'norm_ffn': norm_ffn[l].reshape(1, D_MODEL),
            'w_in_main': w_main, 'w_in_small': w_small,
            'gdn_conv': gdn_conv[l],
            'gdn_alog_vec': _lane_vec([(SM_DECAY, gdn_a_log[l])]),
            'gdn_dtb_vec': _lane_vec([(SM_DECAY, gdn_dt_bias[l])]),
            'gdn_norm': gdn_norm[l], 'q_norm': q_norm[l], 'k_norm': k_norm[l],
            'ssd_conv_w': ssd_conv_w[l], 'ssd_conv_b': ssd_conv_b[l],
            'ssd_alog_vec': _lane_vec([(SM_DT, ssd_a_log[l])]),
            'ssd_dtb_vec': _lane_vec([(SM_DT, ssd_dt_bias[l])]),
            'ssd_d_vec': jnp.repeat(ssd_d[l], SSD_HEAD_DIM).reshape(1, SSD_WIDTH),
            'ssd_norm': ssd_norm[l],
            'w_out': w_out_b[l], 'layer': l,
        }
        mod = mod_all[l]
        y_ctx, (k_l, v_l, g_l, s_l) = _group_layer(
            y_ctx, mod, lambda tm: (lambda i: 0), lp, shared, n_ctx, ctx_len, tm_ctx, ctx_len, 256, None)
        new_k.append(k_l.reshape(n_ctx, ctx_len, ATT_KV_HEADS, HEAD_DIM))
        new_v.append(v_l.reshape(n_ctx, ctx_len, ATT_KV_HEADS, HEAD_DIM))
        new_gdn.append(g_l)
        new_ssd.append(s_l)
        ctx = (cache_k4, cache_v4, state_gdn, state_ssd, l, cos2, sin2)
        y_lat, _ = _group_layer(
            y_lat, mod, lambda tm: (lambda i: 1 + i // (lat_len // tm)), lp, shared, n_lat, lat_len,
            tm_lat, 256, 256, ctx)
    return (y_ctx.reshape(n_ctx, ctx_len, D_MODEL), y_lat.reshape(n_lat, lat_len, D_MODEL),
            jnp.stack(new_k, axis=1), jnp.stack(new_v, axis=1),
            jnp.stack(new_gdn, axis=1), jnp.stack(new_ssd, axis=1))
```
